```python
import jax, jax.numpy as jnp
from jax import lax
import numpy as np

D_MODEL = 1024
BATCH = 4
SEQ = 8192
DEPTH = 1

ATTN_HEADS = 8
HEAD_DIM = D_MODEL // 16
ATTN_WIDTH = ATTN_HEADS * HEAD_DIM
ROPE_DIM = HEAD_DIM // 4
ROPE_THETA = 500000.0
MOBA_BLOCK = 256
MOBA_TOP_K = 3
QUERY_CHUNK = 64
POOL_WINDOWS = (2, 4, 8, 16)
POOL_GROUPS = len(POOL_WINDOWS)
POOL_WIDTH = D_MODEL // 2
POOL_GROUP_DIM = POOL_WIDTH // POOL_GROUPS
N_BRANCHES = 2
IN_WIDTH = 3 * ATTN_WIDTH + POOL_WIDTH + N_BRANCHES * D_MODEL
D_FF = 2816
CONV_WIDTH = 3
EPS = 1e-6
NEG_INF = -1e30

kernel_name = "hybrid_moba_pool_convffn"


def rmsnorm(x, g):
    xf = x.astype(jnp.float32)
    y = xf * lax.rsqrt(jnp.mean(xf * xf, axis=-1, keepdims=True) + EPS)
    return (y * g.astype(jnp.float32)).astype(x.dtype)


def apply_partial_rotary(x, positions):
    half = ROPE_DIM // 2
    inv_freq = ROPE_THETA ** (-jnp.arange(half, dtype=jnp.float32) / half)
    ang = positions.astype(jnp.float32)[:, None] * inv_freq[None, :]
    cos = jnp.cos(ang).astype(x.dtype)
    sin = jnp.sin(ang).astype(x.dtype)
    x1 = x[..., :half]
    x2 = x[..., half:ROPE_DIM]
    return jnp.concatenate([x1 * cos - x2 * sin, x2 * cos + x1 * sin, x[..., ROPE_DIM:]], axis=-1)


def moba_attention(q, k, v):
    B, H, S, Dh = q.shape
    nb = -(-S // MOBA_BLOCK)
    pad = nb * MOBA_BLOCK - S
    k_p = jnp.pad(k, ((0, 0), (0, 0), (0, pad), (0, 0)))
    v_p = jnp.pad(v, ((0, 0), (0, 0), (0, pad), (0, 0)))
    k_blk = k_p.reshape(B, H, nb, MOBA_BLOCK, Dh)
    v_blk = v_p.reshape(B, H, nb, MOBA_BLOCK, Dh)
    k_mean = jnp.mean(k_blk.astype(jnp.float32), axis=3).astype(k.dtype)
    top_k = min(MOBA_TOP_K, nb)
    scale = Dh ** -0.5
    n_chunks = S // QUERY_CHUNK
    gather_blocks = jax.vmap(jax.vmap(lambda kb, idx: kb[idx]))

    def chunk(c):
        start = c * QUERY_CHUNK
        qc = lax.dynamic_slice_in_dim(q, start, QUERY_CHUNK, axis=2)
        q_pos = start + jnp.arange(QUERY_CHUNK)
        own = start // MOBA_BLOCK
        s_blk = jnp.einsum('bhqd,bhnd->bhqn', qc, k_mean).astype(jnp.float32)
        s_blk = jnp.where(jnp.arange(nb) < own, s_blk, -jnp.inf)
        _, sel = lax.top_k(s_blk, top_k)
        sel_valid = sel < own
        k_sel = gather_blocks(k_blk, sel)
        v_sel = gather_blocks(v_blk, sel)
        logit_sel = jnp.einsum('bhqd,bhqkld->bhqkl', qc, k_sel).astype(jnp.float32) * scale
        logit_sel = jnp.where(sel_valid[..., None], logit_sel, NEG_INF)
        k_own = lax.dynamic_index_in_dim(k_blk, own, axis=2, keepdims=False)
        v_own = lax.dynamic_index_in_dim(v_blk, own, axis=2, keepdims=False)
        logit_own = jnp.einsum('bhqd,bhld->bhql', qc, k_own).astype(jnp.float32) * scale
        k_pos_own = own * MOBA_BLOCK + jnp.arange(MOBA_BLOCK)
        logit_own = jnp.where(k_pos_own[None, :] <= q_pos[:, None], logit_own, NEG_INF)
        n_sel = top_k * MOBA_BLOCK
        logits = jnp.concatenate(
            [logit_sel.reshape(B, H, QUERY_CHUNK, n_sel), logit_own], axis=-1)
        p = jax.nn.softmax(logits, axis=-1).astype(v.dtype)
        p_sel = p[..., :n_sel].reshape(B, H, QUERY_CHUNK, top_k, MOBA_BLOCK)
        p_own = p[..., n_sel:]
        return (jnp.einsum('bhqkl,bhqkld->bhqd', p_sel, v_sel)
                + jnp.einsum('bhql,bhld->bhqd', p_own, v_own))

    outs = lax.map(chunk, jnp.arange(n_chunks))
    return outs.transpose(1, 2, 0, 3, 4).reshape(B, H, S, Dh)


def multiscale_pool_mixer(u, w_pool, pool_scale):
    B, S, _ = u.shape
    ug = u.reshape(B, S, POOL_GROUPS, POOL_GROUP_DIM).astype(jnp.float32)
    cs = jnp.cumsum(ug, axis=1)
    t = jnp.arange(S, dtype=jnp.float32)[None, :, None]
    groups = []
    for g, w in enumerate(POOL_WINDOWS):
        c = cs[:, :, g]
        c_prev = jnp.pad(c, ((0, 0), (w, 0), (0, 0)))[:, :S]
        count = jnp.minimum(t + 1.0, float(w))
        groups.append((c - c_prev) / count - ug[:, :, g])
    pooled = jnp.stack(groups, axis=2).astype(u.dtype)
    mixed = jnp.einsum('bsgc,gcd->bsgd', pooled, w_pool)
    return mixed.reshape(B, S, POOL_WIDTH) * pool_scale


def causal_depthwise_conv(u, w, b):
    S = u.shape[1]
    kw = w.shape[0]
    up = jnp.pad(u, ((0, 0), (kw - 1, 0), (0, 0)))
    out = up[:, 0:S] * w[0]
    for j in range(1, kw):
        out = out + up[:, j:j + S] * w[j]
    return out + b


def hybrid_layer(x, norm_mix_g, w_in, b_gate, q_norm_g, k_norm_g, w_pool, pool_scale,
                 w_branch_attn, w_branch_pool, w_out, norm_ffn_g, w_up, conv_w, conv_b, w_down):
    B, S, D = x.shape
    h = rmsnorm(x, norm_mix_g)
    proj = h @ w_in
    q, k, v, u_pool, gate_logits = jnp.split(
        proj, [ATTN_WIDTH, 2 * ATTN_WIDTH, 3 * ATTN_WIDTH, 3 * ATTN_WIDTH + POOL_WIDTH], axis=-1)

    def heads(t):
        return t.reshape(B, S, ATTN_HEADS, HEAD_DIM).transpose(0, 2, 1, 3)

    pos = jnp.arange(S)
    q = apply_partial_rotary(rmsnorm(heads(q), q_norm_g), pos)
    k = apply_partial_rotary(rmsnorm(heads(k), k_norm_g), pos)
    attn = moba_attention(q, k, heads(v)).transpose(0, 2, 1, 3).reshape(B, S, ATTN_WIDTH)
    pooled = multiscale_pool_mixer(u_pool, w_pool, pool_scale)

    gates = jax.nn.sigmoid(gate_logits + b_gate).reshape(B, S, N_BRANCHES, D)
    mixed = gates[:, :, 0] * (attn @ w_branch_attn) + gates[:, :, 1] * (pooled @ w_branch_pool)
    x = x + mixed @ w_out

    h2 = rmsnorm(x, norm_ffn_g)
    up = causal_depthwise_conv(h2 @ w_up, conv_w, conv_b)
    gate, val = jnp.split(up, 2, axis=-1)
    return x + (jax.nn.silu(gate) * val) @ w_down


def setup_inputs(seed: int = 0) -> dict:
    key = jax.random.key(seed)
    ks = jax.random.split(key, 17)
    L = DEPTH
    f32 = jnp.float32

    def nrm(k, shape, scale):
        return jax.random.normal(k, shape, f32) * scale

    return {
        "x": nrm(ks[0], (BATCH, SEQ, D_MODEL), 1.0),
        "norm_mix_g": 1.0 + nrm(ks[1], (L, D_MODEL), 0.02),
        "w_in": nrm(ks[2], (L, D_MODEL, IN_WIDTH), D_MODEL ** -0.5),
        "b_gate": nrm(ks[3], (L, N_BRANCHES * D_MODEL), 0.1),
        "q_norm_g": 1.0 + nrm(ks[4], (L, HEAD_DIM), 0.02),
        "k_norm_g": 1.0 + nrm(ks[5], (L, HEAD_DIM), 0.02),
        "w_pool": nrm(ks[6], (L, POOL_GROUPS, POOL_GROUP_DIM, POOL_GROUP_DIM), POOL_GROUP_DIM ** -0.5),
        "pool_scale": 1.0 + nrm(ks[7], (L, POOL_WIDTH), 0.1),
        "w_branch_attn": nrm(ks[8], (L, ATTN_WIDTH, D_MODEL), ATTN_WIDTH ** -0.5),
        "w_branch_pool": nrm(ks[9], (L, POOL_WIDTH, D_MODEL), POOL_WIDTH ** -0.5),
        "w_out": nrm(ks[10], (L, D_MODEL, D_MODEL), D_MODEL ** -0.5),
        "norm_ffn_g": 1.0 + nrm(ks[11], (L, D_MODEL), 0.02),
        "w_up": nrm(ks[12], (L, D_MODEL, 2 * D_FF), D_MODEL ** -0.5),
        "conv_w": nrm(ks[13], (L, CONV_WIDTH, 2 * D_FF), CONV_WIDTH ** -0.5),
        "conv_b": nrm(ks[14], (L, 2 * D_FF), 0.02),
        "w_down": nrm(ks[15], (L, D_FF, D_MODEL), D_FF ** -0.5),
    }


def reference(x, norm_mix_g, w_in, b_gate, q_norm_g, k_norm_g, w_pool, pool_scale,
              w_branch_attn, w_branch_pool, w_out, norm_ffn_g, w_up, conv_w, conv_b, w_down):
    for l in range(DEPTH):
        x = hybrid_layer(x, norm_mix_g[l], w_in[l], b_gate[l], q_norm_g[l], k_norm_g[l],
                         w_pool[l], pool_scale[l], w_branch_attn[l], w_branch_pool[l], w_out[l],
                         norm_ffn_g[l], w_up[l], conv_w[l], conv_b[l], w_down[l])
    return x
```

```python
import functools

import jax
import jax.numpy as jnp
from jax import lax
from jax.experimental import pallas as pl
from jax.experimental.pallas import tpu as pltpu

ATTN_HEADS = 8
HEAD_DIM = 64
ATTN_WIDTH = ATTN_HEADS * HEAD_DIM
ROPE_DIM = 16
ROPE_HALF = ROPE_DIM // 2
ROPE_THETA = 500000.0
MOBA_BLOCK = 256
MOBA_TOP_K = 3
POOL_WINDOWS = (2, 4, 8, 16)
POOL_WIDTH = 512
POOL_GROUP_DIM = 128
CONV_WIDTH = 3
EPS = 1e-6
NEG_INF = -1e30

LANES = 128
SUBLANES = 8
MXU_DIM = 256
VMEM_LIMIT_BYTES = 56 * 1024 * 1024

HEADS_PER_LANE_TILE = LANES // HEAD_DIM
HEAD_PAIRS = ATTN_HEADS // HEADS_PER_LANE_TILE
POOL_HALO = 16
CONV_HALO = SUBLANES
TOKEN_TILE = 512
FF_CHUNK = 256

BF16 = jnp.bfloat16
F32 = jnp.float32
_NT = (((1,), (1,)), ((), ()))


def _rms(x):
    return x * lax.rsqrt(jnp.mean(x * x, axis=-1, keepdims=True) + EPS)


def _proj_kernel(x_ref, g1_ref, w_ref, wvt_ref, bd_ref, gq_ref, gk_ref, cos_ref, sa_ref, sb_ref,
                 q_ref, k_ref, vt_ref, up_ref, km_ref):
    tm = x_ref.shape[0]
    h = (_rms(x_ref[...]) * g1_ref[...]).astype(BF16)
    proj = jnp.dot(h, w_ref[...], preferred_element_type=F32)
    vt = lax.dot_general(wvt_ref[...], h, _NT, preferred_element_type=F32)
    for blk in range(tm // MOBA_BLOCK):
        vt_ref[:, blk] = vt[:, blk * MOBA_BLOCK:(blk + 1) * MOBA_BLOCK].reshape(
            ATTN_HEADS, HEAD_DIM, MOBA_BLOCK).astype(BF16)
    up_ref[...] = proj[:, 2 * ATTN_WIDTH:]

    cos = cos_ref[...]
    sa = sa_ref[...]
    sb = sb_ref[...]

    def head_norm_rotary(t, g_ref):
        pieces = []
        for c in range(ATTN_WIDTH // MXU_DIM):
            tc = t[:, c * MXU_DIM:(c + 1) * MXU_DIM]
            msq = jnp.dot((tc * tc).astype(BF16), bd_ref[...], preferred_element_type=F32)
            tn = tc * lax.rsqrt(msq + EPS) * g_ref[:, c * MXU_DIM:(c + 1) * MXU_DIM]
            for p in range(MXU_DIM // LANES):
                tp = tn[:, p * LANES:(p + 1) * LANES]
                pieces.append(tp * cos + pltpu.roll(tp, LANES - ROPE_HALF, 1) * sa + pltpu.roll(tp, ROPE_HALF, 1) * sb)
        return pieces

    for p, piece in enumerate(head_norm_rotary(proj[:, :ATTN_WIDTH], gq_ref)):
        q_ref[p] = piece.astype(BF16)
    for p, piece in enumerate(head_norm_rotary(proj[:, ATTN_WIDTH:2 * ATTN_WIDTH], gk_ref)):
        k_ref[p] = piece.astype(BF16)
        for blk in range(tm // MOBA_BLOCK):
            km_ref[blk:blk + 1, p * LANES:(p + 1) * LANES] = jnp.mean(
                piece[blk * MOBA_BLOCK:(blk + 1) * MOBA_BLOCK], axis=0, keepdims=True)


def _attn_kernel(q_ref, k_ref, vt_ref, km_ref, o_ref, bias_ref):
    n_blocks = q_ref.shape[0] // MOBA_BLOCK
    lane = lax.broadcasted_iota(jnp.int32, (1, LANES), 1)
    head_masks = [(lane // HEAD_DIM) == a for a in range(HEADS_PER_LANE_TILE)]
    km_heads = [jnp.where(hm, km_ref[...], 0.0).astype(BF16) for hm in head_masks]
    blk_row = lax.broadcasted_iota(jnp.int32, (n_blocks, MOBA_BLOCK), 0)
    key_pos = lax.broadcasted_iota(jnp.int32, (MOBA_BLOCK, MOBA_BLOCK), 0)
    qry_pos = lax.broadcasted_iota(jnp.int32, (MOBA_BLOCK, MOBA_BLOCK), 1)
    causal = key_pos <= qry_pos

    def query_block(i, carry):
        r0 = pl.multiple_of(i * MOBA_BLOCK, MOBA_BLOCK)
        q_blk = q_ref[pl.ds(r0, MOBA_BLOCK), :]
        k_own = k_ref[pl.ds(r0, MOBA_BLOCK), :]
        q_heads, state = [], []
        for a in range(HEADS_PER_LANE_TILE):
            qa = jnp.where(head_masks[a], q_blk, jnp.zeros_like(q_blk))
            q_heads.append(qa)
            score = lax.dot_general(km_heads[a], qa, _NT, preferred_element_type=F32)
            score = jnp.where(blk_row < i, score, -jnp.inf)
            chosen = jnp.zeros_like(score)
            for _ in range(MOBA_TOP_K):
                best = jnp.max(score, axis=0, keepdims=True)
                first = jnp.min(jnp.where(score == best, blk_row, n_blocks), axis=0, keepdims=True)
                pick = jnp.logical_and(blk_row == first, best > -jnp.inf)
                chosen = jnp.where(pick, 1.0, chosen)
                score = jnp.where(pick, -jnp.inf, score)
            bias_ref[a] = jnp.where(chosen > 0.0, 0.0, NEG_INF)
            s = lax.dot_general(k_own, qa, _NT, preferred_element_type=F32)
            s = jnp.where(causal, s, NEG_INF)
            m = jnp.max(s, axis=0, keepdims=True)
            p = jnp.exp(s - m)
            l = jnp.sum(p, axis=0, keepdims=True)
            acc = jnp.dot(vt_ref[a, i], p.astype(BF16), preferred_element_type=F32)
            state.append((m, l, acc))

        def key_block(j, st):
            c0 = pl.multiple_of(j * MOBA_BLOCK, MOBA_BLOCK)
            k_j = k_ref[pl.ds(c0, MOBA_BLOCK), :]
            new = []
            for a in range(HEADS_PER_LANE_TILE):
                m, l, acc = st[a]
                s = lax.dot_general(k_j, q_heads[a], _NT, preferred_element_type=F32)
                b = bias_ref[a, pl.ds(j, 1), :]
                m_new = jnp.maximum(m, jnp.max(s, axis=0, keepdims=True) + b)
                alpha = jnp.exp(m - m_new)
                p = jnp.exp(s - (m_new - b))
                l = alpha * l + jnp.sum(p, axis=0, keepdims=True)
                acc = alpha * acc + jnp.dot(vt_ref[a, j], p.astype(BF16), preferred_element_type=F32)
                new.append((m_new, l, acc))
            return tuple(new)

        state = lax.fori_loop(0, i, key_block, tuple(state))
        out_t = jnp.concatenate([acc / l for (_, l, acc) in state], axis=0)
        o_ref[pl.ds(r0, MOBA_BLOCK), :] = out_t.T.astype(o_ref.dtype)
        return carry

    lax.fori_loop(0, n_blocks, query_block, 0)


def _mix_kernel(x_ref, attn_ref, u_ref, uh_ref, g1_ref, wg_ref, bg_ref, wp_ref, ps_ref, wba_ref, wbp_ref, wo_ref,
                o_ref, *, tiles_per_seq):
    tm, d = x_ref.shape
    i = pl.program_id(0)
    t_in_seq = (i % tiles_per_seq) * tm
    x = x_ref[...]
    h = (_rms(x) * g1_ref[...]).astype(BF16)
    gates = jax.nn.sigmoid(jnp.dot(h, wg_ref[...], preferred_element_type=F32) + bg_ref[...])

    halo = jnp.where(t_in_seq == 0, 0.0, uh_ref[...])
    ue = jnp.concatenate([halo, u_ref[...]], axis=0)
    pos = (t_in_seq + lax.broadcasted_iota(jnp.int32, (tm, 1), 0)).astype(F32)
    mixed_pool = []
    for g, w in enumerate(POOL_WINDOWS):
        e = ue[:, g * POOL_GROUP_DIM:(g + 1) * POOL_GROUP_DIM]
        win, span = e, 1
        while span < w:
            win = win + pltpu.roll(win, span, 0)
            span *= 2
        count = jnp.minimum(pos + 1.0, float(w))
        pooled = win[POOL_HALO:] / count - e[POOL_HALO:]
        mixed_pool.append(jnp.dot(pooled.astype(BF16), wp_ref[g], preferred_element_type=F32))
    pooled_mix = jnp.concatenate(mixed_pool, axis=1) * ps_ref[...]

    br_attn = jnp.dot(attn_ref[...], wba_ref[...], preferred_element_type=F32)
    br_pool = jnp.dot(pooled_mix.astype(BF16), wbp_ref[...], preferred_element_type=F32)
    mixed = gates[:, :d] * br_attn + gates[:, d:] * br_pool
    o_ref[...] = x + jnp.dot(mixed.astype(BF16), wo_ref[...], preferred_element_type=F32)


def _ffn_kernel(x_ref, xh_ref, g2_ref, wug_ref, wuv_ref, cg_ref, cv_ref, wd_ref, o_ref, *, tiles_per_seq):
    tm = x_ref.shape[0]
    i = pl.program_id(0)
    x = x_ref[...]
    halo = jnp.where(i % tiles_per_seq == 0, 0.0, xh_ref[...])
    h2 = (_rms(jnp.concatenate([halo, x], axis=0)) * g2_ref[...]).astype(BF16)

    def conv(u, cw):
        r = pltpu.roll(u, 2, 0) * cw[0:1] + pltpu.roll(u, 1, 0) * cw[1:2] + u * cw[2:3] + cw[3:4]
        return r[CONV_HALO:]

    acc = jnp.zeros((tm, x_ref.shape[1]), F32)
    for c in range(wug_ref.shape[0]):
        gate = conv(jnp.dot(h2, wug_ref[c], preferred_element_type=F32), cg_ref[c])
        val = conv(jnp.dot(h2, wuv_ref[c], preferred_element_type=F32), cv_ref[c])
        act = (gate * jax.nn.sigmoid(gate) * val).astype(BF16)
        acc = acc + jnp.dot(act, wd_ref[c], preferred_element_type=F32)
    o_ref[...] = x + acc


def _const_spec(shape):
    nd = len(shape)
    return pl.BlockSpec(shape, lambda *_: (0,) * nd, pipeline_mode=pl.Buffered(1))


def _rotary_tables(seq):
    inv_freq = ROPE_THETA ** (-jnp.arange(ROPE_HALF, dtype=F32) / ROPE_HALF)
    ang = jnp.arange(seq).astype(F32)[:, None] * inv_freq[None, :]
    cos, sin = jnp.cos(ang), jnp.sin(ang)
    dim = jnp.arange(LANES) % HEAD_DIM
    freq = dim % ROPE_HALF
    cos_t = jnp.where(dim < ROPE_DIM, cos[:, freq], 1.0)
    sin_next = jnp.where(dim < ROPE_HALF, -sin[:, freq], 0.0)
    sin_prev = jnp.where((dim >= ROPE_HALF) & (dim < ROPE_DIM), sin[:, freq], 0.0)
    return cos_t, sin_next, sin_prev


def _layer(x, norm_mix_g, w_in, b_gate, q_norm_g, k_norm_g, w_pool, pool_scale,
           w_branch_attn, w_branch_pool, w_out, norm_ffn_g, w_up, conv_w, conv_b, w_down):
    B, S, D = x.shape
    N = B * S
    d_ff = w_down.shape[0]
    tm = min(TOKEN_TILE, S)
    assert S % tm == 0 and tm % MOBA_BLOCK == 0 and d_ff % FF_CHUNK == 0
    tiles_per_seq = S // tm
    n_blocks = S // MOBA_BLOCK
    blocks_per_tile = tm // MOBA_BLOCK
    cparams = functools.partial(pltpu.CompilerParams, vmem_limit_bytes=VMEM_LIMIT_BYTES)

    w_qkp = jnp.concatenate([w_in[:, :2 * ATTN_WIDTH], w_in[:, 3 * ATTN_WIDTH:3 * ATTN_WIDTH + POOL_WIDTH]],
                            axis=1).astype(BF16)
    w_vt = w_in[:, 2 * ATTN_WIDTH:3 * ATTN_WIDTH].T.astype(BF16)
    w_gate = w_in[:, 3 * ATTN_WIDTH + POOL_WIDTH:].astype(BF16)
    head_of = jnp.arange(MXU_DIM) // HEAD_DIM
    block_diag = jnp.where(head_of[:, None] == head_of[None, :], 1.0 / HEAD_DIM, 0.0).astype(BF16)
    gq = jnp.tile(q_norm_g * HEAD_DIM ** -0.5, ATTN_HEADS)[None, :]
    gk = jnp.tile(k_norm_g, ATTN_HEADS)[None, :]
    cos_t, sin_next, sin_prev = _rotary_tables(S)

    q, k, vt, u_pool, kmean = pl.pallas_call(
        _proj_kernel,
        grid=(B, tiles_per_seq),
        in_specs=[
            pl.BlockSpec((None, tm, D), lambda b, t: (b, t, 0)),
            _const_spec((1, D)),
            _const_spec(w_qkp.shape),
            _const_spec(w_vt.shape),
            _const_spec(block_diag.shape),
            _const_spec(gq.shape),
            _const_spec(gk.shape),
            pl.BlockSpec((tm, LANES), lambda b, t: (t, 0)),
            pl.BlockSpec((tm, LANES), lambda b, t: (t, 0)),
            pl.BlockSpec((tm, LANES), lambda b, t: (t, 0)),
        ],
        out_specs=[
            pl.BlockSpec((None, HEAD_PAIRS, tm, LANES), lambda b, t: (b, 0, t, 0)),
            pl.BlockSpec((None, HEAD_PAIRS, tm, LANES), lambda b, t: (b, 0, t, 0)),
            pl.BlockSpec((None, ATTN_HEADS, blocks_per_tile, HEAD_DIM, MOBA_BLOCK), lambda b, t: (b, 0, t, 0, 0)),
            pl.BlockSpec((None, tm, POOL_WIDTH), lambda b, t: (b, t, 0)),
            pl.BlockSpec((None, None, blocks_per_tile, ATTN_WIDTH), lambda b, t: (b, t, 0, 0)),
        ],
        out_shape=[
            jax.ShapeDtypeStruct((B, HEAD_PAIRS, S, LANES), BF16),
            jax.ShapeDtypeStruct((B, HEAD_PAIRS, S, LANES), BF16),
            jax.ShapeDtypeStruct((B, ATTN_HEADS, n_blocks, HEAD_DIM, MOBA_BLOCK), BF16),
            jax.ShapeDtypeStruct((B, S, POOL_WIDTH), F32),
            jax.ShapeDtypeStruct((B, tiles_per_seq, blocks_per_tile, ATTN_WIDTH), F32),
        ],
        compiler_params=cparams(dimension_semantics=("parallel", "parallel")),
        name="moba_proj",
    )(x, norm_mix_g[None, :], w_qkp, w_vt, block_diag, gq, gk, cos_t, sin_next, sin_prev)

    kmean = kmean.reshape(B, n_blocks, HEAD_PAIRS, LANES).transpose(0, 2, 1, 3)

    attn = pl.pallas_call(
        _attn_kernel,
        grid=(B, HEAD_PAIRS),
        in_specs=[
            pl.BlockSpec((None, None, S, LANES), lambda b, p: (b, p, 0, 0)),
            pl.BlockSpec((None, None, S, LANES), lambda b, p: (b, p, 0, 0)),
            pl.BlockSpec((None, HEADS_PER_LANE_TILE, n_blocks, HEAD_DIM, MOBA_BLOCK), lambda b, p: (b, p, 0, 0, 0)),
            pl.BlockSpec((None, None, n_blocks, LANES), lambda b, p: (b, p, 0, 0)),
        ],
        out_specs=pl.BlockSpec((None, S, LANES), lambda b, p: (b, 0, p)),
        out_shape=jax.ShapeDtypeStruct((B, S, ATTN_WIDTH), BF16),
        scratch_shapes=[pltpu.VMEM((HEADS_PER_LANE_TILE, n_blocks, MOBA_BLOCK), F32)],
        compiler_params=cparams(dimension_semantics=("parallel", "parallel")),
        name="moba_attn",
    )(q, k, vt, kmean)

    x2 = x.reshape(N, D)
    x1 = pl.pallas_call(
        functools.partial(_mix_kernel, tiles_per_seq=tiles_per_seq),
        grid=(N // tm,),
        in_specs=[
            pl.BlockSpec((tm, D), lambda i: (i, 0)),
            pl.BlockSpec((tm, ATTN_WIDTH), lambda i: (i, 0)),
            pl.BlockSpec((tm, POOL_WIDTH), lambda i: (i, 0)),
            pl.BlockSpec((POOL_HALO, POOL_WIDTH), lambda i: (jnp.maximum(i * (tm // POOL_HALO) - 1, 0), 0)),
            _const_spec((1, D)),
            _const_spec(w_gate.shape),
            _const_spec((1, 2 * D)),
            _const_spec(w_pool.shape),
            _const_spec((1, POOL_WIDTH)),
            _const_spec(w_branch_attn.shape),
            _const_spec(w_branch_pool.shape),
            _const_spec(w_out.shape),
        ],
        out_specs=pl.BlockSpec((tm, D), lambda i: (i, 0)),
        out_shape=jax.ShapeDtypeStruct((N, D), F32),
        compiler_params=cparams(dimension_semantics=("parallel",)),
        name="moba_mix",
    )(x2, attn.reshape(N, ATTN_WIDTH), u_pool.reshape(N, POOL_WIDTH), u_pool.reshape(N, POOL_WIDTH),
      norm_mix_g[None, :], w_gate, b_gate[None, :], w_pool.astype(BF16), pool_scale[None, :],
      w_branch_attn.astype(BF16), w_branch_pool.astype(BF16), w_out.astype(BF16))

    n_chunks = d_ff // FF_CHUNK

    def chunked_cols(w):
        return w.reshape(w.shape[0], n_chunks, FF_CHUNK).transpose(1, 0, 2)

    conv_all = jnp.concatenate([conv_w, conv_b[None, :]], axis=0)
    out = pl.pallas_call(
        functools.partial(_ffn_kernel, tiles_per_seq=tiles_per_seq),
        grid=(N // tm,),
        in_specs=[
            pl.BlockSpec((tm, D), lambda i: (i, 0)),
            pl.BlockSpec((CONV_HALO, D), lambda i: (jnp.maximum(i * (tm // CONV_HALO) - 1, 0), 0)),
            _const_spec((1, D)),
            _const_spec((n_chunks, D, FF_CHUNK)),
            _const_spec((n_chunks, D, FF_CHUNK)),
            _const_spec((n_chunks, CONV_WIDTH + 1, FF_CHUNK)),
            _const_spec((n_chunks, CONV_WIDTH + 1, FF_CHUNK)),
            _const_spec((n_chunks, FF_CHUNK, D)),
        ],
        out_specs=pl.BlockSpec((tm, D), lambda i: (i, 0)),
        out_shape=jax.ShapeDtypeStruct((N, D), F32),
        compiler_params=cparams(dimension_semantics=("parallel",)),
        name="moba_ffn",
    )(x1, x1, norm_ffn_g[None, :],
      chunked_cols(w_up[:, :d_ff]).astype(BF16), chunked_cols(w_up[:, d_ff:]).astype(BF16),
      chunked_cols(conv_all[:, :d_ff]), chunked_cols(conv_all[:, d_ff:]),
      w_down.reshape(n_chunks, FF_CHUNK, D).astype(BF16))
    return out.reshape(B, S, D)


def kernel(x, norm_mix_g, w_in, b_gate, q_norm_g, k_norm_g, w_pool, pool_scale, w_branch_attn, w_branch_pool,
           w_out, norm_ffn_g, w_up, conv_w, conv_b, w_down):
    for l in range(w_in.shape[0]):
        x = _layer(x, norm_mix_g[l], w_in[l], b_gate[l], q_norm_g[l], k_norm_g[l], w_pool[l], pool_scale[l],
                   w_branch_attn[l], w_branch_pool[l], w_out[l], norm_ffn_g[l], w_up[l], conv_w[l], conv_b[l],
                   w_down[l])
    return x
```

```python
import functools
import math

import jax
import jax.numpy as jnp
import numpy as np
from jax import lax
from jax.experimental import pallas as pl
from jax.experimental.pallas import tpu as pltpu

ATTN_HEADS = 8
HEAD_DIM = 64
ATTN_WIDTH = ATTN_HEADS * HEAD_DIM
ROPE_DIM = 16
ROPE_HALF = ROPE_DIM // 2
ROPE_THETA = 500000.0
MOBA_BLOCK = 256
MOBA_TOP_K = 3
POOL_WINDOWS = (2, 4, 8, 16)
POOL_WIDTH = 512
POOL_GROUP_DIM = 128
CONV_WIDTH = 3
EPS = 1e-6
NEG_INF = -1e30

LANES = 128
SUBLANES = 8
MXU_DIM = 256
VMEM_LIMIT_BYTES = 56 * 1024 * 1024

HEADS_PER_LANE_TILE = LANES // HEAD_DIM
HEAD_PAIRS = ATTN_HEADS // HEADS_PER_LANE_TILE
POOL_HALO = 16
CONV_HALO = SUBLANES
TOKEN_TILE = 512
FF_CHUNK = 256
PAIR_GROUP = 4

BF16 = jnp.bfloat16
F32 = jnp.float32
_NT = (((1,), (1,)), ((), ()))


def _rms(x):
    return x * lax.rsqrt(jnp.mean(x * x, axis=-1, keepdims=True) + EPS)


def _proj_kernel(x_ref, g1_ref, w_ref, wvt_ref, bd_ref, gq_ref, gk_ref, cos_ref, sa_ref, sb_ref,
                 q_ref, k_ref, vt_ref, up_ref, km_ref):
    tm = x_ref.shape[0]
    h = (_rms(x_ref[...]) * g1_ref[...]).astype(BF16)
    proj = jnp.dot(h, w_ref[...], preferred_element_type=F32)
    vt = lax.dot_general(wvt_ref[...], h, _NT, preferred_element_type=F32)
    for blk in range(tm // MOBA_BLOCK):
        vt_ref[:, blk] = vt[:, blk * MOBA_BLOCK:(blk + 1) * MOBA_BLOCK].reshape(
            ATTN_HEADS, HEAD_DIM, MOBA_BLOCK).astype(BF16)
    up_ref[...] = proj[:, 2 * ATTN_WIDTH:]

    cos = cos_ref[...]
    sa = sa_ref[...]
    sb = sb_ref[...]

    def head_norm_rotary(t, g_ref):
        pieces = []
        for c in range(ATTN_WIDTH // MXU_DIM):
            tc = t[:, c * MXU_DIM:(c + 1) * MXU_DIM]
            msq = jnp.dot((tc * tc).astype(BF16), bd_ref[...], preferred_element_type=F32)
            tn = tc * lax.rsqrt(msq + EPS) * g_ref[:, c * MXU_DIM:(c + 1) * MXU_DIM]
            for p in range(MXU_DIM // LANES):
                tp = tn[:, p * LANES:(p + 1) * LANES]
                pieces.append(tp * cos + pltpu.roll(tp, LANES - ROPE_HALF, 1) * sa + pltpu.roll(tp, ROPE_HALF, 1) * sb)
        return pieces

    for p, piece in enumerate(head_norm_rotary(proj[:, :ATTN_WIDTH], gq_ref)):
        q_ref[p] = piece.astype(BF16)
    for p, piece in enumerate(head_norm_rotary(proj[:, ATTN_WIDTH:2 * ATTN_WIDTH], gk_ref)):
        k_ref[p] = piece.astype(BF16)
        for blk in range(tm // MOBA_BLOCK):
            km_ref[blk:blk + 1, p * LANES:(p + 1) * LANES] = jnp.mean(
                piece[blk * MOBA_BLOCK:(blk + 1) * MOBA_BLOCK], axis=0, keepdims=True)


def _pair_tables(n_blocks):
    remaining = {i: list(range(i)) for i in range(1, n_blocks)}
    padding = [(0, n_blocks + u) for u in range(PAIR_GROUP)]
    groups = []
    while any(remaining.values()):
        live = sorted((i for i in remaining if remaining[i]), key=lambda i: -len(remaining[i]))[:PAIR_GROUP]
        group = [(remaining[i].pop(), i) for i in live]
        groups.append(group + padding[len(group):])
    if len(groups) % 2:
        groups.append(padding)
    groups.append(padding)
    flat = [pair for group in groups for pair in group]
    return np.asarray([j for j, _ in flat], np.int32), np.asarray([i for _, i in flat], np.int32)


def _attn_kernel(jt_ref, it_ref, q_ref, k_ref, vt_ref, km_ref, o_ref,
                 qa_ref, ka_ref, m_ref, l_ref, acc_ref, s0_ref, s1_ref, c0_ref, c1_ref):
    n_blocks = q_ref.shape[0] // MOBA_BLOCK
    n_groups = jt_ref.shape[0] // PAIR_GROUP - 1
    heads = range(HEADS_PER_LANE_TILE)
    lane = lax.broadcasted_iota(jnp.int32, (1, LANES), 1)
    head_masks = [(lane // HEAD_DIM) == a for a in heads]
    spare_lane0 = [(HEADS_PER_LANE_TILE - 1 - a) * HEAD_DIM for a in heads]
    km_heads = [jnp.where(hm, km_ref[...], 0.0).astype(BF16) for hm in head_masks]
    blk_row = lax.broadcasted_iota(jnp.int32, (n_blocks, MOBA_BLOCK), 0)
    key_pos = lax.broadcasted_iota(jnp.int32, (MOBA_BLOCK, MOBA_BLOCK), 0)
    qry_pos = lax.broadcasted_iota(jnp.int32, (MOBA_BLOCK, MOBA_BLOCK), 1)
    causal = key_pos <= qry_pos
    stat_shape = (SUBLANES, MOBA_BLOCK)

    def init_block(i, carry):
        r0 = pl.multiple_of(i * MOBA_BLOCK, MOBA_BLOCK)
        q_blk = q_ref[pl.ds(r0, MOBA_BLOCK), :]
        k_own = k_ref[pl.ds(r0, MOBA_BLOCK), :]
        for a in heads:
            ka_ref[a, pl.ds(r0, MOBA_BLOCK), :] = jnp.where(
                head_masks[a], k_own, (lane == spare_lane0[a] + i).astype(BF16))
            qa = jnp.where(head_masks[a], q_blk, jnp.zeros_like(q_blk))
            score = lax.dot_general(km_heads[a], qa, _NT, preferred_element_type=F32)
            score = jnp.where(blk_row < i, score, -jnp.inf)
            chosen = jnp.zeros_like(score)
            for _ in range(MOBA_TOP_K):
                best = jnp.max(score, axis=0, keepdims=True)
                first = jnp.min(jnp.where(score == best, blk_row, n_blocks), axis=0, keepdims=True)
                pick = jnp.logical_and(blk_row == first, best > -jnp.inf)
                chosen = jnp.where(pick, 1.0, chosen)
                score = jnp.where(pick, -jnp.inf, score)
            bias = jnp.where(chosen > 0.0, 0.0, NEG_INF)
            bias_rows = [jnp.zeros((spare_lane0[a], MOBA_BLOCK), F32)] if spare_lane0[a] else []
            bias_rows += [bias, jnp.zeros((LANES - spare_lane0[a] - n_blocks, MOBA_BLOCK), F32)]
            bias_lanes = jnp.concatenate(bias_rows, axis=0).T
            qa_ref[a, i] = jnp.where(head_masks[a], q_blk, bias_lanes.astype(BF16))
            s = lax.dot_general(k_own, qa, _NT, preferred_element_type=F32)
            s = jnp.where(causal, s, NEG_INF)
            m = jnp.max(s, axis=0, keepdims=True)
            p = jnp.exp2(s - m)
            m_ref[a, i] = jnp.broadcast_to(m, stat_shape)
            l_ref[a, i] = jnp.broadcast_to(jnp.sum(p, axis=0, keepdims=True), stat_shape)
            acc_ref[a, i] = jnp.dot(vt_ref[a, i], p.astype(BF16), preferred_element_type=F32)
        return carry

    lax.fori_loop(0, n_blocks, init_block, 0)
    for u in range(PAIR_GROUP):
        for a in heads:
            qa_ref[a, n_blocks + u] = jnp.zeros(qa_ref.shape[2:], qa_ref.dtype)
            m_ref[a, n_blocks + u] = jnp.zeros(stat_shape, F32)
            l_ref[a, n_blocks + u] = jnp.zeros(stat_shape, F32)
            acc_ref[a, n_blocks + u] = jnp.zeros(acc_ref.shape[2:], F32)

    def stage_scores(g, s_ref, c_ref):
        for u in range(PAIR_GROUP):
            t = g * PAIR_GROUP + u
            j, i = jt_ref[t], it_ref[t]
            c0 = pl.multiple_of(j * MOBA_BLOCK, MOBA_BLOCK)
            for a in heads:
                s = lax.dot_general(ka_ref[a, pl.ds(c0, MOBA_BLOCK), :], qa_ref[a, i], _NT,
                                    preferred_element_type=F32)
                s_ref[u, a] = s
                c_ref[u, a] = jnp.broadcast_to(jnp.max(s, axis=0, keepdims=True), stat_shape)

    def fold_scores(g, s_ref, c_ref):
        loaded = []
        for u in range(PAIR_GROUP):
            t = g * PAIR_GROUP + u
            j, i = jt_ref[t], it_ref[t]
            for a in heads:
                loaded.append((u, a, i, j, m_ref[a, i], l_ref[a, i], acc_ref[a, i]))
        updated = []
        for u, a, i, j, m, l, acc in loaded:
            m_new = jnp.maximum(m, c_ref[u, a])
            m_ref[a, i] = m_new
            alpha = jnp.exp2(m - m_new)
            p = jnp.exp2(s_ref[u, a] - m_new[0:1])
            l_new = alpha * l + jnp.sum(p, axis=0, keepdims=True)
            acc_new = alpha[0:1] * acc + jnp.dot(vt_ref[a, j], p.astype(BF16), preferred_element_type=F32)
            updated.append((a, i, l_new, acc_new))
        for a, i, l_new, acc_new in updated:
            l_ref[a, i] = l_new
            acc_ref[a, i] = acc_new

    stage_scores(0, s0_ref, c0_ref)

    def sweep_two_groups(h, carry):
        g = 2 * h
        stage_scores(g + 1, s1_ref, c1_ref)
        fold_scores(g, s0_ref, c0_ref)
        stage_scores(g + 2, s0_ref, c0_ref)
        fold_scores(g + 1, s1_ref, c1_ref)
        return carry

    lax.fori_loop(0, n_groups // 2, sweep_two_groups, 0)

    def finish_block(i, carry):
        r0 = pl.multiple_of(i * MOBA_BLOCK, MOBA_BLOCK)
        out_t = jnp.concatenate([acc_ref[a, i] / l_ref[a, i][0:1] for a in heads], axis=0)
        o_ref[pl.ds(r0, MOBA_BLOCK), :] = out_t.T.astype(o_ref.dtype)
        return carry

    lax.fori_loop(0, n_blocks, finish_block, 0)


def _mix_kernel(x_ref, attn_ref, u_ref, uh_ref, g1_ref, wg_ref, bg_ref, wp_ref, ps_ref, wba_ref, wbp_ref, wo_ref,
                o_ref, *, tiles_per_seq):
    tm, d = x_ref.shape
    i = pl.program_id(0)
    t_in_seq = (i % tiles_per_seq) * tm
    x = x_ref[...]
    h = (_rms(x) * g1_ref[...]).astype(BF16)
    gates = jax.nn.sigmoid(jnp.dot(h, wg_ref[...], preferred_element_type=F32) + bg_ref[...])

    halo = jnp.where(t_in_seq == 0, 0.0, uh_ref[...])
    ue = jnp.concatenate([halo, u_ref[...]], axis=0)
    pos = (t_in_seq + lax.broadcasted_iota(jnp.int32, (tm, 1), 0)).astype(F32)
    mixed_pool = []
    for g, w in enumerate(POOL_WINDOWS):
        e = ue[:, g * POOL_GROUP_DIM:(g + 1) * POOL_GROUP_DIM]
        win, span = e, 1
        while span < w:
            win = win + pltpu.roll(win, span, 0)
            span *= 2
        count = jnp.minimum(pos + 1.0, float(w))
        pooled = win[POOL_HALO:] / count - e[POOL_HALO:]
        mixed_pool.append(jnp.dot(pooled.astype(BF16), wp_ref[g], preferred_element_type=F32))
    pooled_mix = jnp.concatenate(mixed_pool, axis=1) * ps_ref[...]

    br_attn = jnp.dot(attn_ref[...], wba_ref[...], preferred_element_type=F32)
    br_pool = jnp.dot(pooled_mix.astype(BF16), wbp_ref[...], preferred_element_type=F32)
    mixed = gates[:, :d] * br_attn + gates[:, d:] * br_pool
    o_ref[...] = x + jnp.dot(mixed.astype(BF16), wo_ref[...], preferred_element_type=F32)


def _ffn_kernel(x_ref, xh_ref, g2_ref, wug_ref, wuv_ref, cg_ref, cv_ref, wd_ref, o_ref, *, tiles_per_seq):
    tm = x_ref.shape[0]
    i = pl.program_id(0)
    x = x_ref[...]
    halo = jnp.where(i % tiles_per_seq == 0, 0.0, xh_ref[...])
    h2 = (_rms(jnp.concatenate([halo, x], axis=0)) * g2_ref[...]).astype(BF16)

    def conv(u, cw):
        r = pltpu.roll(u, 2, 0) * cw[0:1] + pltpu.roll(u, 1, 0) * cw[1:2] + u * cw[2:3] + cw[3:4]
        return r[CONV_HALO:]

    acc = jnp.zeros((tm, x_ref.shape[1]), F32)
    for c in range(wug_ref.shape[0]):
        gate = conv(jnp.dot(h2, wug_ref[c], preferred_element_type=F32), cg_ref[c])
        val = conv(jnp.dot(h2, wuv_ref[c], preferred_element_type=F32), cv_ref[c])
        act = (gate * jax.nn.sigmoid(gate) * val).astype(BF16)
        acc = acc + jnp.dot(act, wd_ref[c], preferred_element_type=F32)
    o_ref[...] = x + acc


def _const_spec(shape):
    nd = len(shape)
    return pl.BlockSpec(shape, lambda *_: (0,) * nd, pipeline_mode=pl.Buffered(1))


def _rotary_tables(seq):
    inv_freq = ROPE_THETA ** (-jnp.arange(ROPE_HALF, dtype=F32) / ROPE_HALF)
    ang = jnp.arange(seq).astype(F32)[:, None] * inv_freq[None, :]
    cos, sin = jnp.cos(ang), jnp.sin(ang)
    dim = jnp.arange(LANES) % HEAD_DIM
    freq = dim % ROPE_HALF
    cos_t = jnp.where(dim < ROPE_DIM, cos[:, freq], 1.0)
    sin_next = jnp.where(dim < ROPE_HALF, -sin[:, freq], 0.0)
    sin_prev = jnp.where((dim >= ROPE_HALF) & (dim < ROPE_DIM), sin[:, freq], 0.0)
    return cos_t, sin_next, sin_prev


def _layer(x, norm_mix_g, w_in, b_gate, q_norm_g, k_norm_g, w_pool, pool_scale,
           w_branch_attn, w_branch_pool, w_out, norm_ffn_g, w_up, conv_w, conv_b, w_down):
    B, S, D = x.shape
    N = B * S
    d_ff = w_down.shape[0]
    tm = min(TOKEN_TILE, S)
    assert S % tm == 0 and tm % MOBA_BLOCK == 0 and d_ff % FF_CHUNK == 0
    assert S // MOBA_BLOCK <= HEAD_DIM
    tiles_per_seq = S // tm
    n_blocks = S // MOBA_BLOCK
    blocks_per_tile = tm // MOBA_BLOCK
    cparams = functools.partial(pltpu.CompilerParams, vmem_limit_bytes=VMEM_LIMIT_BYTES)

    w_qkp = jnp.concatenate([w_in[:, :2 * ATTN_WIDTH], w_in[:, 3 * ATTN_WIDTH:3 * ATTN_WIDTH + POOL_WIDTH]],
                            axis=1).astype(BF16)
    w_vt = w_in[:, 2 * ATTN_WIDTH:3 * ATTN_WIDTH].T.astype(BF16)
    w_gate = w_in[:, 3 * ATTN_WIDTH + POOL_WIDTH:].astype(BF16)
    head_of = jnp.arange(MXU_DIM) // HEAD_DIM
    block_diag = jnp.where(head_of[:, None] == head_of[None, :], 1.0 / HEAD_DIM, 0.0).astype(BF16)
    gq = jnp.tile(q_norm_g * (HEAD_DIM ** -0.5 * math.log2(math.e)), ATTN_HEADS)[None, :]
    gk = jnp.tile(k_norm_g, ATTN_HEADS)[None, :]
    cos_t, sin_next, sin_prev = _rotary_tables(S)

    q, k, vt, u_pool, kmean = pl.pallas_call(
        _proj_kernel,
        grid=(B, tiles_per_seq),
        in_specs=[
            pl.BlockSpec((None, tm, D), lambda b, t: (b, t, 0)),
            _const_spec((1, D)),
            _const_spec(w_qkp.shape),
            _const_spec(w_vt.shape),
            _const_spec(block_diag.shape),
            _const_spec(gq.shape),
            _const_spec(gk.shape),
            pl.BlockSpec((tm, LANES), lambda b, t: (t, 0)),
            pl.BlockSpec((tm, LANES), lambda b, t: (t, 0)),
            pl.BlockSpec((tm, LANES), lambda b, t: (t, 0)),
        ],
        out_specs=[
            pl.BlockSpec((None, HEAD_PAIRS, tm, LANES), lambda b, t: (b, 0, t, 0)),
            pl.BlockSpec((None, HEAD_PAIRS, tm, LANES), lambda b, t: (b, 0, t, 0)),
            pl.BlockSpec((None, ATTN_HEADS, blocks_per_tile, HEAD_DIM, MOBA_BLOCK), lambda b, t: (b, 0, t, 0, 0)),
            pl.BlockSpec((None, tm, POOL_WIDTH), lambda b, t: (b, t, 0)),
            pl.BlockSpec((None, None, blocks_per_tile, ATTN_WIDTH), lambda b, t: (b, t, 0, 0)),
        ],
        out_shape=[
            jax.ShapeDtypeStruct((B, HEAD_PAIRS, S, LANES), BF16),
            jax.ShapeDtypeStruct((B, HEAD_PAIRS, S, LANES), BF16),
            jax.ShapeDtypeStruct((B, ATTN_HEADS, n_blocks, HEAD_DIM, MOBA_BLOCK), BF16),
            jax.ShapeDtypeStruct((B, S, POOL_WIDTH), F32),
            jax.ShapeDtypeStruct((B, tiles_per_seq, blocks_per_tile, ATTN_WIDTH), F32),
        ],
        compiler_params=cparams(dimension_semantics=("parallel", "parallel")),
        name="moba_proj",
    )(x, norm_mix_g[None, :], w_qkp, w_vt, block_diag, gq, gk, cos_t, sin_next, sin_prev)

    kmean = kmean.reshape(B, n_blocks, HEAD_PAIRS, LANES).transpose(0, 2, 1, 3)

    key_tab, query_tab = _pair_tables(n_blocks)
    n_slots = n_blocks + PAIR_GROUP
    attn = pl.pallas_call(
        _attn_kernel,
        grid_spec=pltpu.PrefetchScalarGridSpec(
            num_scalar_prefetch=2,
            grid=(B, HEAD_PAIRS),
            in_specs=[
                pl.BlockSpec((None, None, S, LANES), lambda b, p, *_: (b, p, 0, 0)),
                pl.BlockSpec((None, None, S, LANES), lambda b, p, *_: (b, p, 0, 0)),
                pl.BlockSpec((None, HEADS_PER_LANE_TILE, n_blocks, HEAD_DIM, MOBA_BLOCK),
                             lambda b, p, *_: (b, p, 0, 0, 0)),
                pl.BlockSpec((None, None, n_blocks, LANES), lambda b, p, *_: (b, p, 0, 0)),
            ],
            out_specs=pl.BlockSpec((None, S, LANES), lambda b, p, *_: (b, 0, p)),
            scratch_shapes=[
                pltpu.VMEM((HEADS_PER_LANE_TILE, n_slots, MOBA_BLOCK, LANES), BF16),
                pltpu.VMEM((HEADS_PER_LANE_TILE, S, LANES), BF16),
                pltpu.VMEM((HEADS_PER_LANE_TILE, n_slots, SUBLANES, MOBA_BLOCK), F32),
                pltpu.VMEM((HEADS_PER_LANE_TILE, n_slots, SUBLANES, MOBA_BLOCK), F32),
                pltpu.VMEM((HEADS_PER_LANE_TILE, n_slots, HEAD_DIM, MOBA_BLOCK), F32),
            ] + 2 * [pltpu.VMEM((PAIR_GROUP, HEADS_PER_LANE_TILE, MOBA_BLOCK, MOBA_BLOCK), F32)]
              + 2 * [pltpu.VMEM((PAIR_GROUP, HEADS_PER_LANE_TILE, SUBLANES, MOBA_BLOCK), F32)],
        ),
        out_shape=jax.ShapeDtypeStruct((B, S, ATTN_WIDTH), BF16),
        compiler_params=cparams(dimension_semantics=("parallel", "parallel")),
        name="moba_attn",
    )(key_tab, query_tab, q, k, vt, kmean)

    x2 = x.reshape(N, D)
    x1 = pl.pallas_call(
        functools.partial(_mix_kernel, tiles_per_seq=tiles_per_seq),
        grid=(N // tm,),
        in_specs=[
            pl.BlockSpec((tm, D), lambda i: (i, 0)),
            pl.BlockSpec((tm, ATTN_WIDTH), lambda i: (i, 0)),
            pl.BlockSpec((tm, POOL_WIDTH), lambda i: (i, 0)),
            pl.BlockSpec((POOL_HALO, POOL_WIDTH), lambda i: (jnp.maximum(i * (tm // POOL_HALO) - 1, 0), 0)),
            _const_spec((1, D)),
            _const_spec(w_gate.shape),
            _const_spec((1, 2 * D)),
            _const_spec(w_pool.shape),
            _const_spec((1, POOL_WIDTH)),
            _const_spec(w_branch_attn.shape),
            _const_spec(w_branch_pool.shape),
            _const_spec(w_out.shape),
        ],
        out_specs=pl.BlockSpec((tm, D), lambda i: (i, 0)),
        out_shape=jax.ShapeDtypeStruct((N, D), F32),
        compiler_params=cparams(dimension_semantics=("parallel",)),
        name="moba_mix",
    )(x2, attn.reshape(N, ATTN_WIDTH), u_pool.reshape(N, POOL_WIDTH), u_pool.reshape(N, POOL_WIDTH),
      norm_mix_g[None, :], w_gate, b_gate[None, :], w_pool.astype(BF16), pool_scale[None, :],
      w_branch_attn.astype(BF16), w_branch_pool.astype(BF16), w_out.astype(BF16))

    n_chunks = d_ff // FF_CHUNK

    def chunked_cols(w):
        return w.reshape(w.shape[0], n_chunks, FF_CHUNK).transpose(1, 0, 2)

    conv_all = jnp.concatenate([conv_w, conv_b[None, :]], axis=0)
    out = pl.pallas_call(
        functools.partial(_ffn_kernel, tiles_per_seq=tiles_per_seq),
        grid=(N // tm,),
        in_specs=[
            pl.BlockSpec((tm, D), lambda i: (i, 0)),
            pl.BlockSpec((CONV_HALO, D), lambda i: (jnp.maximum(i * (tm // CONV_HALO) - 1, 0), 0)),
            _const_spec((1, D)),
            _const_spec((n_chunks, D, FF_CHUNK)),
            _const_spec((n_chunks, D, FF_CHUNK)),
            _const_spec((n_chunks, CONV_WIDTH + 1, FF_CHUNK)),
            _const_spec((n_chunks, CONV_WIDTH + 1, FF_CHUNK)),
            _const_spec((n_chunks, FF_CHUNK, D)),
        ],
        out_specs=pl.BlockSpec((tm, D), lambda i: (i, 0)),
        out_shape=jax.ShapeDtypeStruct((N, D), F32),
        compiler_params=cparams(dimension_semantics=("parallel",)),
        name="moba_ffn",
    )(x1, x1, norm_ffn_g[None, :],
      chunked_cols(w_up[:, :d_ff]).astype(BF16), chunked_cols(w_up[:, d_ff:]).astype(BF16),
      chunked_cols(conv_all[:, :d_ff]), chunked_cols(conv_all[:, d_ff:]),
      w_down.reshape(n_chunks, FF_CHUNK, D).astype(BF16))
    return out.reshape(B, S, D)


def kernel(x, norm_mix_g, w_in, b_gate, q_norm_g, k_norm_g, w_pool, pool_scale, w_branch_attn, w_branch_pool,
           w_out, norm_ffn_g, w_up, conv_w, conv_b, w_down):
    for l in range(w_in.shape[0]):
        x = _layer(x, norm_mix_g[l], w_in[l], b_gate[l], q_norm_g[l], k_norm_g[l], w_pool[l], pool_scale[l],
                   w_branch_attn[l], w_branch_pool[l], w_out[l], norm_ffn_g[l], w_up[l], conv_w[l], conv_b[l],
                   w_down[l])
    return x
```

```python
import functools
import math

import jax
import jax.numpy as jnp
import numpy as np
from jax import lax
from jax.experimental import pallas as pl
from jax.experimental.pallas import tpu as pltpu

ATTN_HEADS = 8
HEAD_DIM = 64
ATTN_WIDTH = ATTN_HEADS * HEAD_DIM
ROPE_DIM = 16
ROPE_HALF = ROPE_DIM // 2
ROPE_THETA = 500000.0
MOBA_BLOCK = 256
MOBA_TOP_K = 3
POOL_WINDOWS = (2, 4, 8, 16)
POOL_WIDTH = 512
POOL_GROUP_DIM = 128
CONV_WIDTH = 3
EPS = 1e-6
NEG_INF = -1e30

LANES = 128
SUBLANES = 8
MXU_DIM = 256
VMEM_LIMIT_BYTES = 56 * 1024 * 1024

HEADS_PER_LANE_TILE = LANES // HEAD_DIM
HEAD_PAIRS = ATTN_HEADS // HEADS_PER_LANE_TILE
POOL_HALO = 16
CONV_HALO = SUBLANES
BF16_SUBLANES = 16
V_ROWS = HEAD_DIM + BF16_SUBLANES
STATE_ROWS = HEAD_DIM + SUBLANES
TOKEN_TILE = 512
FF_CHUNK = 256
DOWN_CHUNKS = 11
PAIR_GROUP = 16

BF16 = jnp.bfloat16
F32 = jnp.float32
_NT = (((1,), (1,)), ((), ()))


def _rms(x):
    return x * lax.rsqrt(jnp.mean(x * x, axis=-1, keepdims=True) + EPS)


def _proj_kernel(x_ref, g1_ref, w_ref, wvt_ref, bd_ref, gq_ref, gk_ref, cos_ref, sa_ref, sb_ref,
                 q_ref, k_ref, vt_ref, up_ref, km_ref):
    tm = x_ref.shape[0]
    h = (_rms(x_ref[...]) * g1_ref[...]).astype(BF16)
    proj = jnp.dot(h, w_ref[...], preferred_element_type=F32)
    vt = lax.dot_general(wvt_ref[...], h, _NT, preferred_element_type=F32)
    for blk in range(tm // MOBA_BLOCK):
        vt_ref[:, blk] = jnp.concatenate(
            [vt[:, blk * MOBA_BLOCK:(blk + 1) * MOBA_BLOCK].reshape(ATTN_HEADS, HEAD_DIM, MOBA_BLOCK),
             jnp.ones((ATTN_HEADS, SUBLANES, MOBA_BLOCK), F32),
             jnp.zeros((ATTN_HEADS, V_ROWS - STATE_ROWS, MOBA_BLOCK), F32)], axis=1).astype(BF16)
    up_ref[...] = proj[:, 2 * ATTN_WIDTH:]

    cos = cos_ref[...]
    sa = sa_ref[...]
    sb = sb_ref[...]

    def head_norm_rotary(t, g_ref):
        pieces = []
        for c in range(ATTN_WIDTH // MXU_DIM):
            tc = t[:, c * MXU_DIM:(c + 1) * MXU_DIM]
            msq = jnp.dot((tc * tc).astype(BF16), bd_ref[...], preferred_element_type=F32)
            tn = tc * lax.rsqrt(msq + EPS) * g_ref[:, c * MXU_DIM:(c + 1) * MXU_DIM]
            for p in range(MXU_DIM // LANES):
                tp = tn[:, p * LANES:(p + 1) * LANES]
                pieces.append(tp * cos + pltpu.roll(tp, LANES - ROPE_HALF, 1) * sa + pltpu.roll(tp, ROPE_HALF, 1) * sb)
        return pieces

    for p, piece in enumerate(head_norm_rotary(proj[:, :ATTN_WIDTH], gq_ref)):
        q_ref[p] = piece.astype(BF16)
    for p, piece in enumerate(head_norm_rotary(proj[:, ATTN_WIDTH:2 * ATTN_WIDTH], gk_ref)):
        k_ref[p] = piece.astype(BF16)
        for blk in range(tm // MOBA_BLOCK):
            km_ref[blk:blk + 1, p * LANES:(p + 1) * LANES] = jnp.mean(
                piece[blk * MOBA_BLOCK:(blk + 1) * MOBA_BLOCK], axis=0, keepdims=True)


def _pair_tables(n_blocks):
    remaining = {i: list(range(i)) for i in range(1, n_blocks)}
    padding = [(0, n_blocks + u) for u in range(PAIR_GROUP)]
    groups = []
    while any(remaining.values()):
        live = sorted((i for i in remaining if remaining[i]), key=lambda i: -len(remaining[i]))[:PAIR_GROUP]
        group = [(remaining[i].pop(), i) for i in live]
        groups.append(group + padding[len(group):])
    groups.append(padding)
    flat = [pair for group in groups for pair in group]
    return np.asarray([j for j, _ in flat], np.int32), np.asarray([i for _, i in flat], np.int32)


def _attn_kernel(jt_ref, it_ref, q_ref, k_ref, vt_ref, km_ref, o_ref,
                 qa_ref, ka_ref, m_ref, acc_ref, s0_ref, s1_ref, c0_ref, c1_ref):
    n_blocks = q_ref.shape[0] // MOBA_BLOCK
    n_groups = jt_ref.shape[0] // PAIR_GROUP - 1
    heads = range(HEADS_PER_LANE_TILE)
    lane = lax.broadcasted_iota(jnp.int32, (1, LANES), 1)
    head_masks = [(lane // HEAD_DIM) == a for a in heads]
    spare_lane0 = [(HEADS_PER_LANE_TILE - 1 - a) * HEAD_DIM for a in heads]
    km_heads = [jnp.where(hm, km_ref[...], 0.0).astype(BF16) for hm in head_masks]
    blk_row = lax.broadcasted_iota(jnp.int32, (n_blocks, MOBA_BLOCK), 0)
    key_pos = lax.broadcasted_iota(jnp.int32, (MOBA_BLOCK, MOBA_BLOCK), 0)
    qry_pos = lax.broadcasted_iota(jnp.int32, (MOBA_BLOCK, MOBA_BLOCK), 1)
    causal = key_pos <= qry_pos
    def tiles(s):
        return s.reshape(s.shape[0] // SUBLANES, SUBLANES, s.shape[1])

    def over_keys(reduce_pair, s3):
        r = s3[0]
        for t in range(1, s3.shape[0]):
            r = reduce_pair(r, s3[t])
        for shift in (4, 2, 1):
            r = reduce_pair(r, pltpu.roll(r, shift, 0))
        return r

    def init_block(i, carry):
        r0 = pl.multiple_of(i * MOBA_BLOCK, MOBA_BLOCK)
        q_blk = q_ref[pl.ds(r0, MOBA_BLOCK), :]
        k_own = k_ref[pl.ds(r0, MOBA_BLOCK), :]
        for a in heads:
            ka_ref[a, pl.ds(r0, MOBA_BLOCK), :] = jnp.where(
                head_masks[a], k_own, (lane == spare_lane0[a] + i).astype(BF16))
            qa = jnp.where(head_masks[a], q_blk, jnp.zeros_like(q_blk))
            score = lax.dot_general(km_heads[a], qa, _NT, preferred_element_type=F32)
            score = jnp.where(blk_row < i, score, -jnp.inf)
            chosen = jnp.zeros_like(score)
            for _ in range(MOBA_TOP_K):
                best = jnp.max(score, axis=0, keepdims=True)
                first = jnp.min(jnp.where(score == best, blk_row, n_blocks), axis=0, keepdims=True)
                pick = jnp.logical_and(blk_row == first, best > -jnp.inf)
                chosen = jnp.where(pick, 1.0, chosen)
                score = jnp.where(pick, -jnp.inf, score)
            bias = jnp.where(chosen > 0.0, 0.0, NEG_INF)
            bias_rows = [jnp.zeros((spare_lane0[a], MOBA_BLOCK), F32)] if spare_lane0[a] else []
            bias_rows += [bias, jnp.zeros((LANES - spare_lane0[a] - n_blocks, MOBA_BLOCK), F32)]
            bias_lanes = jnp.concatenate(bias_rows, axis=0).T
            qa_ref[a, i] = jnp.where(head_masks[a], q_blk, bias_lanes.astype(BF16))
            s = lax.dot_general(k_own, qa, _NT, preferred_element_type=F32)
            s3 = tiles(jnp.where(causal, s, NEG_INF))
            m = over_keys(jnp.maximum, s3)
            p3 = jnp.exp2(s3 - m[None])
            m_ref[a, i] = m
            acc_ref[a, i] = jnp.dot(vt_ref[a, i], p3.reshape(MOBA_BLOCK, MOBA_BLOCK).astype(BF16),
                                    preferred_element_type=F32)[:STATE_ROWS]
        return carry

    lax.fori_loop(0, n_blocks, init_block, 0)
    for u in range(PAIR_GROUP):
        for a in heads:
            qa_ref[a, n_blocks + u] = jnp.zeros(qa_ref.shape[2:], qa_ref.dtype)
            m_ref[a, n_blocks + u] = jnp.zeros(m_ref.shape[2:], F32)
            acc_ref[a, n_blocks + u] = jnp.zeros(acc_ref.shape[2:], F32)

    def stage_scores(g, u, a, s_ref, c_ref):
        t = g * PAIR_GROUP + u
        c0 = pl.multiple_of(jt_ref[t] * MOBA_BLOCK, MOBA_BLOCK)
        s3 = tiles(lax.dot_general(ka_ref[a, pl.ds(c0, MOBA_BLOCK), :], qa_ref[a, it_ref[t]], _NT,
                                   preferred_element_type=F32))
        s_ref[u, a] = s3
        c_ref[u, a] = over_keys(jnp.maximum, s3)

    def fold_scores(g, u, a, s_ref, c_ref):
        t = g * PAIR_GROUP + u
        j, i = jt_ref[t], it_ref[t]
        m = m_ref[a, i]
        m_new = jnp.maximum(m, c_ref[u, a])
        m_ref[a, i] = m_new
        alpha = jnp.exp2(m - m_new)
        p3 = jnp.exp2(s_ref[u, a] - m_new[None])
        pv = jnp.dot(vt_ref[a, j], p3.reshape(MOBA_BLOCK, MOBA_BLOCK).astype(BF16),
                     preferred_element_type=F32)[:STATE_ROWS]
        acc_ref[a, i] = (alpha[None] * tiles(acc_ref[a, i])).reshape(pv.shape) + pv

    staging = ((s0_ref, c0_ref), (s1_ref, c1_ref))
    for u in range(PAIR_GROUP):
        for a in heads:
            stage_scores(0, u, a, *staging[0])

    def sweep_group(g, carry):
        for parity in range(2):
            @pl.when(g % 2 == parity)
            def _():
                for u in range(PAIR_GROUP):
                    for a in heads:
                        stage_scores(g + 1, u, a, *staging[1 - parity])
                        fold_scores(g, u, a, *staging[parity])
        return carry

    lax.fori_loop(0, n_groups, sweep_group, 0)

    def finish_block(i, carry):
        r0 = pl.multiple_of(i * MOBA_BLOCK, MOBA_BLOCK)
        out_t = jnp.concatenate(
            [(tiles(acc_ref[a, i, :HEAD_DIM]) / acc_ref[a, i, HEAD_DIM:][None]).reshape(HEAD_DIM, MOBA_BLOCK)
             for a in heads], axis=0)
        o_ref[pl.ds(r0, MOBA_BLOCK), :] = out_t.T.astype(o_ref.dtype)
        return carry

    lax.fori_loop(0, n_blocks, finish_block, 0)


def _mix_kernel(x_ref, attn_ref, u_ref, uh_ref, g1_ref, wg_ref, bg_ref, wp_ref, ps_ref, wba_ref, wbp_ref, wo_ref,
                o_ref, *, tiles_per_seq):
    tm, d = x_ref.shape
    i = pl.program_id(0)
    t_in_seq = (i % tiles_per_seq) * tm
    x = x_ref[...]
    h = (_rms(x) * g1_ref[...]).astype(BF16)
    gates = jax.nn.sigmoid(jnp.dot(h, wg_ref[...], preferred_element_type=F32) + bg_ref[...])

    halo = jnp.where(t_in_seq == 0, 0.0, uh_ref[...])
    ue = jnp.concatenate([halo, u_ref[...]], axis=0)
    pos = (t_in_seq + lax.broadcasted_iota(jnp.int32, (tm, 1), 0)).astype(F32)
    mixed_pool = []
    for g, w in enumerate(POOL_WINDOWS):
        e = ue[:, g * POOL_GROUP_DIM:(g + 1) * POOL_GROUP_DIM]
        win, span = e, 1
        while span < w:
            win = win + pltpu.roll(win, span, 0)
            span *= 2
        count = jnp.minimum(pos + 1.0, float(w))
        pooled = win[POOL_HALO:] / count - e[POOL_HALO:]
        mixed_pool.append(jnp.dot(pooled.astype(BF16), wp_ref[g], preferred_element_type=F32))
    pooled_mix = jnp.concatenate(mixed_pool, axis=1) * ps_ref[...]

    br_attn = jnp.dot(attn_ref[...], wba_ref[...], preferred_element_type=F32)
    br_pool = jnp.dot(pooled_mix.astype(BF16), wbp_ref[...], preferred_element_type=F32)
    mixed = gates[:, :d] * br_attn + gates[:, d:] * br_pool
    o_ref[...] = x + jnp.dot(mixed.astype(BF16), wo_ref[...], preferred_element_type=F32)


def _ffn_kernel(x_ref, xh_ref, g2_ref, wug_ref, wuv_ref, cg_ref, cv_ref, wd_ref, o_ref,
                ug0_ref, uv0_ref, ug1_ref, uv1_ref, act_ref, *, tiles_per_seq):
    tm = x_ref.shape[0]
    n_chunks = wug_ref.shape[0]
    i = pl.program_id(0)
    x = x_ref[...]
    halo = jnp.where(i % tiles_per_seq == 0, 0.0, xh_ref[...])
    h2 = (_rms(jnp.concatenate([halo, x], axis=0)) * g2_ref[...]).astype(BF16)
    staging = [(ug0_ref, uv0_ref), (ug1_ref, uv1_ref)]

    def stage_up(c):
        ug_ref, uv_ref = staging[c % 2]
        ug_ref[...] = jnp.dot(h2, wug_ref[c], preferred_element_type=F32)
        uv_ref[...] = jnp.dot(h2, wuv_ref[c], preferred_element_type=F32)

    def conv(u_ref, cw):
        taps = [u_ref[pl.ds(CONV_HALO - (CONV_WIDTH - 1) + j, tm), :] for j in range(CONV_WIDTH)]
        return taps[0] * cw[0:1] + taps[1] * cw[1:2] + taps[2] * cw[2:3] + cw[3:4]

    stage_up(0)
    out, done = x, 0
    for c in range(n_chunks):
        if c + 1 < n_chunks:
            stage_up(c + 1)
        ug_ref, uv_ref = staging[c % 2]
        gate = conv(ug_ref, cg_ref[c])
        val = conv(uv_ref, cv_ref[c])
        act_ref[:, c * FF_CHUNK:(c + 1) * FF_CHUNK] = (gate * jax.nn.sigmoid(gate) * val).astype(BF16)
        if (c + 1) % DOWN_CHUNKS == 0 or c + 1 == n_chunks:
            lo, hi = done * FF_CHUNK, (c + 1) * FF_CHUNK
            out = out + jnp.dot(act_ref[:, lo:hi], wd_ref[lo:hi, :], preferred_element_type=F32)
            done = c + 1
    o_ref[...] = out


def _const_spec(shape):
    nd = len(shape)
    return pl.BlockSpec(shape, lambda *_: (0,) * nd, pipeline_mode=pl.Buffered(1))


def _rotary_tables(seq):
    inv_freq = ROPE_THETA ** (-jnp.arange(ROPE_HALF, dtype=F32) / ROPE_HALF)
    ang = jnp.arange(seq).astype(F32)[:, None] * inv_freq[None, :]
    cos, sin = jnp.cos(ang), jnp.sin(ang)
    dim = jnp.arange(LANES) % HEAD_DIM
    freq = dim % ROPE_HALF
    cos_t = jnp.where(dim < ROPE_DIM, cos[:, freq], 1.0)
    sin_next = jnp.where(dim < ROPE_HALF, -sin[:, freq], 0.0)
    sin_prev = jnp.where((dim >= ROPE_HALF) & (dim < ROPE_DIM), sin[:, freq], 0.0)
    return cos_t, sin_next, sin_prev


def _layer(x, norm_mix_g, w_in, b_gate, q_norm_g, k_norm_g, w_pool, pool_scale,
           w_branch_attn, w_branch_pool, w_out, norm_ffn_g, w_up, conv_w, conv_b, w_down):
    B, S, D = x.shape
    N = B * S
    d_ff = w_down.shape[0]
    tm = min(TOKEN_TILE, S)
    assert S % tm == 0 and tm % MOBA_BLOCK == 0 and d_ff % FF_CHUNK == 0
    assert S // MOBA_BLOCK <= HEAD_DIM
    tiles_per_seq = S // tm
    n_blocks = S // MOBA_BLOCK
    blocks_per_tile = tm // MOBA_BLOCK
    cparams = functools.partial(pltpu.CompilerParams, vmem_limit_bytes=VMEM_LIMIT_BYTES)

    w_qkp = jnp.concatenate([w_in[:, :2 * ATTN_WIDTH], w_in[:, 3 * ATTN_WIDTH:3 * ATTN_WIDTH + POOL_WIDTH]],
                            axis=1).astype(BF16)
    w_vt = w_in[:, 2 * ATTN_WIDTH:3 * ATTN_WIDTH].T.astype(BF16)
    w_gate = w_in[:, 3 * ATTN_WIDTH + POOL_WIDTH:].astype(BF16)
    head_of = jnp.arange(MXU_DIM) // HEAD_DIM
    block_diag = jnp.where(head_of[:, None] == head_of[None, :], 1.0 / HEAD_DIM, 0.0).astype(BF16)
    gq = jnp.tile(q_norm_g * (HEAD_DIM ** -0.5 * math.log2(math.e)), ATTN_HEADS)[None, :]
    gk = jnp.tile(k_norm_g, ATTN_HEADS)[None, :]
    cos_t, sin_next, sin_prev = _rotary_tables(S)

    q, k, vt, u_pool, kmean = pl.pallas_call(
        _proj_kernel,
        grid=(B, tiles_per_seq),
        in_specs=[
            pl.BlockSpec((None, tm, D), lambda b, t: (b, t, 0)),
            _const_spec((1, D)),
            _const_spec(w_qkp.shape),
            _const_spec(w_vt.shape),
            _const_spec(block_diag.shape),
            _const_spec(gq.shape),
            _const_spec(gk.shape),
            pl.BlockSpec((tm, LANES), lambda b, t: (t, 0)),
            pl.BlockSpec((tm, LANES), lambda b, t: (t, 0)),
            pl.BlockSpec((tm, LANES), lambda b, t: (t, 0)),
        ],
        out_specs=[
            pl.BlockSpec((None, HEAD_PAIRS, tm, LANES), lambda b, t: (b, 0, t, 0)),
            pl.BlockSpec((None, HEAD_PAIRS, tm, LANES), lambda b, t: (b, 0, t, 0)),
            pl.BlockSpec((None, ATTN_HEADS, blocks_per_tile, V_ROWS, MOBA_BLOCK), lambda b, t: (b, 0, t, 0, 0)),
            pl.BlockSpec((None, tm, POOL_WIDTH), lambda b, t: (b, t, 0)),
            pl.BlockSpec((None, None, blocks_per_tile, ATTN_WIDTH), lambda b, t: (b, t, 0, 0)),
        ],
        out_shape=[
            jax.ShapeDtypeStruct((B, HEAD_PAIRS, S, LANES), BF16),
            jax.ShapeDtypeStruct((B, HEAD_PAIRS, S, LANES), BF16),
            jax.ShapeDtypeStruct((B, ATTN_HEADS, n_blocks, V_ROWS, MOBA_BLOCK), BF16),
            jax.ShapeDtypeStruct((B, S, POOL_WIDTH), F32),
            jax.ShapeDtypeStruct((B, tiles_per_seq, blocks_per_tile, ATTN_WIDTH), F32),
        ],
        compiler_params=cparams(dimension_semantics=("parallel", "parallel")),
        name="moba_proj",
    )(x, norm_mix_g[None, :], w_qkp, w_vt, block_diag, gq, gk, cos_t, sin_next, sin_prev)

    kmean = kmean.reshape(B, n_blocks, HEAD_PAIRS, LANES).transpose(0, 2, 1, 3)

    key_tab, query_tab = _pair_tables(n_blocks)
    n_slots = n_blocks + PAIR_GROUP
    attn = pl.pallas_call(
        _attn_kernel,
        grid_spec=pltpu.PrefetchScalarGridSpec(
            num_scalar_prefetch=2,
            grid=(B, HEAD_PAIRS),
            in_specs=[
                pl.BlockSpec((None, None, S, LANES), lambda b, p, *_: (b, p, 0, 0)),
                pl.BlockSpec((None, None, S, LANES), lambda b, p, *_: (b, p, 0, 0)),
                pl.BlockSpec((None, HEADS_PER_LANE_TILE, n_blocks, V_ROWS, MOBA_BLOCK),
                             lambda b, p, *_: (b, p, 0, 0, 0)),
                pl.BlockSpec((None, None, n_blocks, LANES), lambda b, p, *_: (b, p, 0, 0)),
            ],
            out_specs=pl.BlockSpec((None, S, LANES), lambda b, p, *_: (b, 0, p)),
            scratch_shapes=[
                pltpu.VMEM((HEADS_PER_LANE_TILE, n_slots, MOBA_BLOCK, LANES), BF16),
                pltpu.VMEM((HEADS_PER_LANE_TILE, S, LANES), BF16),
                pltpu.VMEM((HEADS_PER_LANE_TILE, n_slots, SUBLANES, MOBA_BLOCK), F32),
                pltpu.VMEM((HEADS_PER_LANE_TILE, n_slots, STATE_ROWS, MOBA_BLOCK), F32),
            ] + 2 * [pltpu.VMEM((PAIR_GROUP, HEADS_PER_LANE_TILE, MOBA_BLOCK // SUBLANES, SUBLANES, MOBA_BLOCK),
                                F32)]
              + 2 * [pltpu.VMEM((PAIR_GROUP, HEADS_PER_LANE_TILE, SUBLANES, MOBA_BLOCK), F32)],
        ),
        out_shape=jax.ShapeDtypeStruct((B, S, ATTN_WIDTH), BF16),
        compiler_params=cparams(dimension_semantics=("parallel", "parallel")),
        name="moba_attn",
    )(key_tab, query_tab, q, k, vt, kmean)

    x2 = x.reshape(N, D)
    x1 = pl.pallas_call(
        functools.partial(_mix_kernel, tiles_per_seq=tiles_per_seq),
        grid=(N // tm,),
        in_specs=[
            pl.BlockSpec((tm, D), lambda i: (i, 0)),
            pl.BlockSpec((tm, ATTN_WIDTH), lambda i: (i, 0)),
            pl.BlockSpec((tm, POOL_WIDTH), lambda i: (i, 0)),
            pl.BlockSpec((POOL_HALO, POOL_WIDTH), lambda i: (jnp.maximum(i * (tm // POOL_HALO) - 1, 0), 0)),
            _const_spec((1, D)),
            _const_spec(w_gate.shape),
            _const_spec((1, 2 * D)),
            _const_spec(w_pool.shape),
            _const_spec((1, POOL_WIDTH)),
            _const_spec(w_branch_attn.shape),
            _const_spec(w_branch_pool.shape),
            _const_spec(w_out.shape),
        ],
        out_specs=pl.BlockSpec((tm, D), lambda i: (i, 0)),
        out_shape=jax.ShapeDtypeStruct((N, D), F32),
        compiler_params=cparams(dimension_semantics=("parallel",)),
        name="moba_mix",
    )(x2, attn.reshape(N, ATTN_WIDTH), u_pool.reshape(N, POOL_WIDTH), u_pool.reshape(N, POOL_WIDTH),
      norm_mix_g[None, :], w_gate, b_gate[None, :], w_pool.astype(BF16), pool_scale[None, :],
      w_branch_attn.astype(BF16), w_branch_pool.astype(BF16), w_out.astype(BF16))

    n_chunks = d_ff // FF_CHUNK

    def chunked_cols(w):
        return w.reshape(w.shape[0], n_chunks, FF_CHUNK).transpose(1, 0, 2)

    conv_all = jnp.concatenate([conv_w, conv_b[None, :]], axis=0)
    out = pl.pallas_call(
        functools.partial(_ffn_kernel, tiles_per_seq=tiles_per_seq),
        grid=(N // tm,),
        in_specs=[
            pl.BlockSpec((tm, D), lambda i: (i, 0)),
            pl.BlockSpec((CONV_HALO, D), lambda i: (jnp.maximum(i * (tm // CONV_HALO) - 1, 0), 0)),
            _const_spec((1, D)),
            _const_spec((n_chunks, D, FF_CHUNK)),
            _const_spec((n_chunks, D, FF_CHUNK)),
            _const_spec((n_chunks, CONV_WIDTH + 1, FF_CHUNK)),
            _const_spec((n_chunks, CONV_WIDTH + 1, FF_CHUNK)),
            _const_spec((d_ff, D)),
        ],
        out_specs=pl.BlockSpec((tm, D), lambda i: (i, 0)),
        out_shape=jax.ShapeDtypeStruct((N, D), F32),
        scratch_shapes=4 * [pltpu.VMEM((CONV_HALO + tm, FF_CHUNK), F32)]
                       + [pltpu.VMEM((tm, d_ff), BF16)],
        compiler_params=cparams(dimension_semantics=("parallel",)),
        name="moba_ffn",
    )(x1, x1, norm_ffn_g[None, :],
      chunked_cols(w_up[:, :d_ff]).astype(BF16), chunked_cols(w_up[:, d_ff:]).astype(BF16),
      chunked_cols(conv_all[:, :d_ff]), chunked_cols(conv_all[:, d_ff:]),
      w_down.astype(BF16))
    return out.reshape(B, S, D)


def kernel(x, norm_mix_g, w_in, b_gate, q_norm_g, k_norm_g, w_pool, pool_scale, w_branch_attn, w_branch_pool,
           w_out, norm_ffn_g, w_up, conv_w, conv_b, w_down):
    for l in range(w_in.shape[0]):
        x = _layer(x, norm_mix_g[l], w_in[l], b_gate[l], q_norm_g[l], k_norm_g[l], w_pool[l], pool_scale[l],
                   w_branch_attn[l], w_branch_pool[l], w_out[l], norm_ffn_g[l], w_up[l], conv_w[l], conv_b[l],
                   w_down[l])
    return x
```

```python
import functools
import math

import jax
import jax.numpy as jnp
import numpy as np
from jax import lax
from jax.experimental import pallas as pl
from jax.experimental.pallas import tpu as pltpu

ATTN_HEADS = 8
HEAD_DIM = 64
ATTN_WIDTH = ATTN_HEADS * HEAD_DIM
ROPE_DIM = 16
ROPE_HALF = ROPE_DIM // 2
ROPE_THETA = 500000.0
MOBA_BLOCK = 256
MOBA_TOP_K = 3
POOL_WINDOWS = (2, 4, 8, 16)
POOL_WIDTH = 512
POOL_GROUP_DIM = 128
CONV_WIDTH = 3
EPS = 1e-6
NEG_INF = -1e30

LANES = 128
SUBLANES = 8
MXU_DIM = 256
VMEM_LIMIT_BYTES = 56 * 1024 * 1024

HEADS_PER_LANE_TILE = LANES // HEAD_DIM
HEAD_PAIRS = ATTN_HEADS // HEADS_PER_LANE_TILE
POOL_HALO = 16
CONV_HALO = SUBLANES
BF16_SUBLANES = 16
V_ROWS = HEAD_DIM + BF16_SUBLANES
STATE_ROWS = HEAD_DIM + SUBLANES
TOKEN_TILE = 512
FFN_TILE = 512
FF_CHUNK = 256
INIT_UNROLL = 4
DOWN_LAG = 2
PAIR_GROUP = 16

BF16 = jnp.bfloat16
F32 = jnp.float32
_NT = (((1,), (1,)), ((), ()))


def _rms(x):
    return x * lax.rsqrt(jnp.mean(x * x, axis=-1, keepdims=True) + EPS)


def _proj_kernel(x_ref, g1_ref, w_ref, wvt_ref, bd_ref, gq_ref, gk_ref, cos_ref, sa_ref, sb_ref,
                 q_ref, k_ref, vt_ref, up_ref, km_ref):
    tm = x_ref.shape[0]
    h = (_rms(x_ref[...]) * g1_ref[...]).astype(BF16)
    proj = jnp.dot(h, w_ref[...], preferred_element_type=F32)
    vt = lax.dot_general(wvt_ref[...], h, _NT, preferred_element_type=F32)
    for blk in range(tm // MOBA_BLOCK):
        vt_ref[:, blk] = jnp.concatenate(
            [vt[:, blk * MOBA_BLOCK:(blk + 1) * MOBA_BLOCK].reshape(ATTN_HEADS, HEAD_DIM, MOBA_BLOCK),
             jnp.ones((ATTN_HEADS, SUBLANES, MOBA_BLOCK), F32),
             jnp.zeros((ATTN_HEADS, V_ROWS - STATE_ROWS, MOBA_BLOCK), F32)], axis=1).astype(BF16)
    up_ref[...] = proj[:, 2 * ATTN_WIDTH:]

    cos = cos_ref[...]
    sa = sa_ref[...]
    sb = sb_ref[...]

    def head_norm_rotary(t, g_ref):
        pieces = []
        for c in range(ATTN_WIDTH // MXU_DIM):
            tc = t[:, c * MXU_DIM:(c + 1) * MXU_DIM]
            msq = jnp.dot((tc * tc).astype(BF16), bd_ref[...], preferred_element_type=F32)
            tn = tc * lax.rsqrt(msq + EPS) * g_ref[:, c * MXU_DIM:(c + 1) * MXU_DIM]
            for p in range(MXU_DIM // LANES):
                tp = tn[:, p * LANES:(p + 1) * LANES]
                pieces.append(tp * cos + pltpu.roll(tp, LANES - ROPE_HALF, 1) * sa + pltpu.roll(tp, ROPE_HALF, 1) * sb)
        return pieces

    for p, piece in enumerate(head_norm_rotary(proj[:, :ATTN_WIDTH], gq_ref)):
        q_ref[p] = piece.astype(BF16)
    for p, piece in enumerate(head_norm_rotary(proj[:, ATTN_WIDTH:2 * ATTN_WIDTH], gk_ref)):
        k_ref[p] = piece.astype(BF16)
        for blk in range(tm // MOBA_BLOCK):
            km_ref[blk:blk + 1, p * LANES:(p + 1) * LANES] = jnp.mean(
                piece[blk * MOBA_BLOCK:(blk + 1) * MOBA_BLOCK], axis=0, keepdims=True)


def _pair_tables(n_blocks):
    remaining = {i: list(range(i)) for i in range(1, n_blocks)}
    padding = [(0, n_blocks + u) for u in range(PAIR_GROUP)]
    groups = []
    while any(remaining.values()):
        live = sorted((i for i in remaining if remaining[i]), key=lambda i: -len(remaining[i]))[:PAIR_GROUP]
        group = [(remaining[i].pop(), i) for i in live]
        groups.append(group + padding[len(group):])
    groups.append(padding)
    flat = [pair for group in groups for pair in group]
    return np.asarray([j for j, _ in flat], np.int32), np.asarray([i for _, i in flat], np.int32)


def _attn_kernel(jt_ref, it_ref, q_ref, k_ref, vt_ref, km_ref, o_ref,
                 qa_ref, ka_ref, m_ref, acc_ref, s0_ref, s1_ref, c0_ref, c1_ref):
    n_blocks = q_ref.shape[0] // MOBA_BLOCK
    n_groups = jt_ref.shape[0] // PAIR_GROUP - 1
    heads = range(HEADS_PER_LANE_TILE)
    lane = lax.broadcasted_iota(jnp.int32, (1, LANES), 1)
    head_masks = [(lane // HEAD_DIM) == a for a in heads]
    spare_lane0 = [(HEADS_PER_LANE_TILE - 1 - a) * HEAD_DIM for a in heads]
    km_heads = [jnp.where(hm, km_ref[...], 0.0).astype(BF16) for hm in head_masks]
    blk_row = lax.broadcasted_iota(jnp.int32, (n_blocks, MOBA_BLOCK), 0)
    key_pos = lax.broadcasted_iota(jnp.int32, (MOBA_BLOCK, MOBA_BLOCK), 0)
    qry_pos = lax.broadcasted_iota(jnp.int32, (MOBA_BLOCK, MOBA_BLOCK), 1)
    causal = key_pos <= qry_pos
    def tiles(s):
        return s.reshape(s.shape[0] // SUBLANES, SUBLANES, s.shape[1])

    def over_keys(reduce_pair, s3):
        r = s3[0]
        for t in range(1, s3.shape[0]):
            r = reduce_pair(r, s3[t])
        for shift in (4, 2, 1):
            r = reduce_pair(r, pltpu.roll(r, shift, 0))
        return r

    def init_block(i):
        r0 = pl.multiple_of(i * MOBA_BLOCK, MOBA_BLOCK)
        q_blk = q_ref[pl.ds(r0, MOBA_BLOCK), :]
        k_own = k_ref[pl.ds(r0, MOBA_BLOCK), :]
        for a in heads:
            ka_ref[a, pl.ds(r0, MOBA_BLOCK), :] = jnp.where(
                head_masks[a], k_own, (lane == spare_lane0[a] + i).astype(BF16))
            qa = jnp.where(head_masks[a], q_blk, jnp.zeros_like(q_blk))
            score = lax.dot_general(km_heads[a], qa, _NT, preferred_element_type=F32)
            score = jnp.where(blk_row < i, score, -jnp.inf)
            chosen = jnp.zeros_like(score)
            for _ in range(MOBA_TOP_K):
                best = jnp.max(score, axis=0, keepdims=True)
                first = jnp.min(jnp.where(score == best, blk_row, n_blocks), axis=0, keepdims=True)
                pick = jnp.logical_and(blk_row == first, best > -jnp.inf)
                chosen = jnp.where(pick, 1.0, chosen)
                score = jnp.where(pick, -jnp.inf, score)
            bias = jnp.where(chosen > 0.0, 0.0, NEG_INF)
            bias_rows = [jnp.zeros((spare_lane0[a], MOBA_BLOCK), F32)] if spare_lane0[a] else []
            bias_rows += [bias, jnp.zeros((LANES - spare_lane0[a] - n_blocks, MOBA_BLOCK), F32)]
            bias_lanes = jnp.concatenate(bias_rows, axis=0).T
            qa_ref[a, i] = jnp.where(head_masks[a], q_blk, bias_lanes.astype(BF16))
            s = lax.dot_general(k_own, qa, _NT, preferred_element_type=F32)
            s3 = tiles(jnp.where(causal, s, NEG_INF))
            m = over_keys(jnp.maximum, s3)
            p3 = jnp.exp2(s3 - m[None])
            m_ref[a, i] = m
            acc_ref[a, i] = jnp.dot(vt_ref[a, i], p3.reshape(MOBA_BLOCK, MOBA_BLOCK).astype(BF16),
                                    preferred_element_type=F32)[:STATE_ROWS]

    def init_blocks(h, carry):
        for d in range(INIT_UNROLL):
            init_block(h * INIT_UNROLL + d)
        return carry

    lax.fori_loop(0, n_blocks // INIT_UNROLL, init_blocks, 0)
    for u in range(PAIR_GROUP):
        for a in heads:
            qa_ref[a, n_blocks + u] = jnp.zeros(qa_ref.shape[2:], qa_ref.dtype)
            m_ref[a, n_blocks + u] = jnp.zeros(m_ref.shape[2:], F32)
            acc_ref[a, n_blocks + u] = jnp.zeros(acc_ref.shape[2:], F32)

    def stage_scores(g, u, a, s_ref, c_ref):
        t = g * PAIR_GROUP + u
        c0 = pl.multiple_of(jt_ref[t] * MOBA_BLOCK, MOBA_BLOCK)
        s3 = tiles(lax.dot_general(ka_ref[a, pl.ds(c0, MOBA_BLOCK), :], qa_ref[a, it_ref[t]], _NT,
                                   preferred_element_type=F32))
        s_ref[u, a] = s3
        c_ref[u, a] = over_keys(jnp.maximum, s3)

    def fold_scores(g, u, a, s_ref, c_ref):
        t = g * PAIR_GROUP + u
        j, i = jt_ref[t], it_ref[t]
        m = m_ref[a, i]
        m_new = jnp.maximum(m, c_ref[u, a])
        m_ref[a, i] = m_new
        alpha = jnp.exp2(m - m_new)
        p3 = jnp.exp2(s_ref[u, a] - m_new[None])
        pv = jnp.dot(vt_ref[a, j], p3.reshape(MOBA_BLOCK, MOBA_BLOCK).astype(BF16),
                     preferred_element_type=F32)[:STATE_ROWS]
        acc_ref[a, i] = (alpha[None] * tiles(acc_ref[a, i])).reshape(pv.shape) + pv

    staging = ((s0_ref, c0_ref), (s1_ref, c1_ref))
    for u in range(PAIR_GROUP):
        for a in heads:
            stage_scores(0, u, a, *staging[0])

    def sweep_group(g, carry):
        for parity in range(2):
            @pl.when(g % 2 == parity)
            def _():
                for u in range(PAIR_GROUP):
                    for a in heads:
                        stage_scores(g + 1, u, a, *staging[1 - parity])
                        fold_scores(g, u, a, *staging[parity])
        return carry

    lax.fori_loop(0, n_groups, sweep_group, 0)

    def finish_block(i, carry):
        r0 = pl.multiple_of(i * MOBA_BLOCK, MOBA_BLOCK)
        out_t = jnp.concatenate(
            [(tiles(acc_ref[a, i, :HEAD_DIM]) / acc_ref[a, i, HEAD_DIM:][None]).reshape(HEAD_DIM, MOBA_BLOCK)
             for a in heads], axis=0)
        o_ref[pl.ds(r0, MOBA_BLOCK), :] = out_t.T.astype(o_ref.dtype)
        return carry

    lax.fori_loop(0, n_blocks, finish_block, 0)


def _mix_kernel(x_ref, attn_ref, u_ref, uh_ref, g1_ref, wg_ref, bg_ref, wp_ref, ps_ref, wba_ref, wbp_ref, wo_ref,
                o_ref, *, tiles_per_seq):
    tm, d = x_ref.shape
    i = pl.program_id(0)
    t_in_seq = (i % tiles_per_seq) * tm
    x = x_ref[...]
    h = (_rms(x) * g1_ref[...]).astype(BF16)
    gates = jax.nn.sigmoid(jnp.dot(h, wg_ref[...], preferred_element_type=F32) + bg_ref[...])

    halo = jnp.where(t_in_seq == 0, 0.0, uh_ref[...])
    ue = jnp.concatenate([halo, u_ref[...]], axis=0)
    pos = (t_in_seq + lax.broadcasted_iota(jnp.int32, (tm, 1), 0)).astype(F32)
    mixed_pool = []
    for g, w in enumerate(POOL_WINDOWS):
        e = ue[:, g * POOL_GROUP_DIM:(g + 1) * POOL_GROUP_DIM]
        win, span = e, 1
        while span < w:
            win = win + pltpu.roll(win, span, 0)
            span *= 2
        count = jnp.minimum(pos + 1.0, float(w))
        pooled = win[POOL_HALO:] / count - e[POOL_HALO:]
        mixed_pool.append(jnp.dot(pooled.astype(BF16), wp_ref[g], preferred_element_type=F32))
    pooled_mix = jnp.concatenate(mixed_pool, axis=1) * ps_ref[...]

    br_attn = jnp.dot(attn_ref[...], wba_ref[...], preferred_element_type=F32)
    br_pool = jnp.dot(pooled_mix.astype(BF16), wbp_ref[...], preferred_element_type=F32)
    mixed = gates[:, :d] * br_attn + gates[:, d:] * br_pool
    o_ref[...] = x + jnp.dot(mixed.astype(BF16), wo_ref[...], preferred_element_type=F32)


def _ffn_kernel(x_ref, xh_ref, g2_ref, wug_ref, wuv_ref, cg_ref, cv_ref, wd_ref, o_ref,
                ug0_ref, uv0_ref, ug1_ref, uv1_ref, act_ref, *, tiles_per_seq):
    tm = x_ref.shape[0]
    n_chunks = wug_ref.shape[0]
    i = pl.program_id(0)
    x = x_ref[...]
    halo = jnp.where(i % tiles_per_seq == 0, 0.0, xh_ref[...])
    h2 = (_rms(jnp.concatenate([halo, x], axis=0)) * g2_ref[...]).astype(BF16)
    staging = [(ug0_ref, uv0_ref), (ug1_ref, uv1_ref)]

    def stage_up(c):
        ug_ref, uv_ref = staging[c % 2]
        ug_ref[...] = jnp.dot(h2, wug_ref[c], preferred_element_type=F32)
        uv_ref[...] = jnp.dot(h2, wuv_ref[c], preferred_element_type=F32)

    def conv(u_ref, cw):
        taps = [u_ref[pl.ds(CONV_HALO - (CONV_WIDTH - 1) + j, tm), :] for j in range(CONV_WIDTH)]
        return taps[0] * cw[0:1] + taps[1] * cw[1:2] + taps[2] * cw[2:3] + cw[3:4]

    def down(c):
        lo, hi = c * FF_CHUNK, (c + 1) * FF_CHUNK
        o_ref[...] += jnp.dot(act_ref[:, lo:hi], wd_ref[lo:hi, :], preferred_element_type=F32)

    o_ref[...] = x
    stage_up(0)
    for c in range(n_chunks):
        if c + 1 < n_chunks:
            stage_up(c + 1)
        if c >= DOWN_LAG:
            down(c - DOWN_LAG)
        ug_ref, uv_ref = staging[c % 2]
        gate = conv(ug_ref, cg_ref[c])
        val = conv(uv_ref, cv_ref[c])
        act_ref[:, c * FF_CHUNK:(c + 1) * FF_CHUNK] = (gate * jax.nn.sigmoid(gate) * val).astype(BF16)
    for c in range(max(n_chunks - DOWN_LAG, 0), n_chunks):
        down(c)


def _const_spec(shape):
    nd = len(shape)
    return pl.BlockSpec(shape, lambda *_: (0,) * nd, pipeline_mode=pl.Buffered(1))


def _rotary_tables(seq):
    inv_freq = ROPE_THETA ** (-jnp.arange(ROPE_HALF, dtype=F32) / ROPE_HALF)
    ang = jnp.arange(seq).astype(F32)[:, None] * inv_freq[None, :]
    cos, sin = jnp.cos(ang), jnp.sin(ang)
    dim = jnp.arange(LANES) % HEAD_DIM
    freq = dim % ROPE_HALF
    cos_t = jnp.where(dim < ROPE_DIM, cos[:, freq], 1.0)
    sin_next = jnp.where(dim < ROPE_HALF, -sin[:, freq], 0.0)
    sin_prev = jnp.where((dim >= ROPE_HALF) & (dim < ROPE_DIM), sin[:, freq], 0.0)
    return cos_t, sin_next, sin_prev


def _layer(x, norm_mix_g, w_in, b_gate, q_norm_g, k_norm_g, w_pool, pool_scale,
           w_branch_attn, w_branch_pool, w_out, norm_ffn_g, w_up, conv_w, conv_b, w_down):
    B, S, D = x.shape
    N = B * S
    d_ff = w_down.shape[0]
    tm = min(TOKEN_TILE, S)
    assert S % tm == 0 and tm % MOBA_BLOCK == 0 and d_ff % FF_CHUNK == 0
    assert S // MOBA_BLOCK <= HEAD_DIM
    assert (S // MOBA_BLOCK) % INIT_UNROLL == 0
    tiles_per_seq = S // tm
    n_blocks = S // MOBA_BLOCK
    blocks_per_tile = tm // MOBA_BLOCK
    cparams = functools.partial(pltpu.CompilerParams, vmem_limit_bytes=VMEM_LIMIT_BYTES)

    w_qkp = jnp.concatenate([w_in[:, :2 * ATTN_WIDTH], w_in[:, 3 * ATTN_WIDTH:3 * ATTN_WIDTH + POOL_WIDTH]],
                            axis=1).astype(BF16)
    w_vt = w_in[:, 2 * ATTN_WIDTH:3 * ATTN_WIDTH].T.astype(BF16)
    w_gate = w_in[:, 3 * ATTN_WIDTH + POOL_WIDTH:].astype(BF16)
    head_of = jnp.arange(MXU_DIM) // HEAD_DIM
    block_diag = jnp.where(head_of[:, None] == head_of[None, :], 1.0 / HEAD_DIM, 0.0).astype(BF16)
    gq = jnp.tile(q_norm_g * (HEAD_DIM ** -0.5 * math.log2(math.e)), ATTN_HEADS)[None, :]
    gk = jnp.tile(k_norm_g, ATTN_HEADS)[None, :]
    cos_t, sin_next, sin_prev = _rotary_tables(S)

    q, k, vt, u_pool, kmean = pl.pallas_call(
        _proj_kernel,
        grid=(B, tiles_per_seq),
        in_specs=[
            pl.BlockSpec((None, tm, D), lambda b, t: (b, t, 0)),
            _const_spec((1, D)),
            _const_spec(w_qkp.shape),
            _const_spec(w_vt.shape),
            _const_spec(block_diag.shape),
            _const_spec(gq.shape),
            _const_spec(gk.shape),
            pl.BlockSpec((tm, LANES), lambda b, t: (t, 0)),
            pl.BlockSpec((tm, LANES), lambda b, t: (t, 0)),
            pl.BlockSpec((tm, LANES), lambda b, t: (t, 0)),
        ],
        out_specs=[
            pl.BlockSpec((None, HEAD_PAIRS, tm, LANES), lambda b, t: (b, 0, t, 0)),
            pl.BlockSpec((None, HEAD_PAIRS, tm, LANES), lambda b, t: (b, 0, t, 0)),
            pl.BlockSpec((None, ATTN_HEADS, blocks_per_tile, V_ROWS, MOBA_BLOCK), lambda b, t: (b, 0, t, 0, 0)),
            pl.BlockSpec((None, tm, POOL_WIDTH), lambda b, t: (b, t, 0)),
            pl.BlockSpec((None, None, blocks_per_tile, ATTN_WIDTH), lambda b, t: (b, t, 0, 0)),
        ],
        out_shape=[
            jax.ShapeDtypeStruct((B, HEAD_PAIRS, S, LANES), BF16),
            jax.ShapeDtypeStruct((B, HEAD_PAIRS, S, LANES), BF16),
            jax.ShapeDtypeStruct((B, ATTN_HEADS, n_blocks, V_ROWS, MOBA_BLOCK), BF16),
            jax.ShapeDtypeStruct((B, S, POOL_WIDTH), F32),
            jax.ShapeDtypeStruct((B, tiles_per_seq, blocks_per_tile, ATTN_WIDTH), F32),
        ],
        compiler_params=cparams(dimension_semantics=("parallel", "parallel")),
        name="moba_proj",
    )(x, norm_mix_g[None, :], w_qkp, w_vt, block_diag, gq, gk, cos_t, sin_next, sin_prev)

    kmean = kmean.reshape(B, n_blocks, HEAD_PAIRS, LANES).transpose(0, 2, 1, 3)

    key_tab, query_tab = _pair_tables(n_blocks)
    n_slots = n_blocks + PAIR_GROUP
    attn = pl.pallas_call(
        _attn_kernel,
        grid_spec=pltpu.PrefetchScalarGridSpec(
            num_scalar_prefetch=2,
            grid=(B, HEAD_PAIRS),
            in_specs=[
                pl.BlockSpec((None, None, S, LANES), lambda b, p, *_: (b, p, 0, 0)),
                pl.BlockSpec((None, None, S, LANES), lambda b, p, *_: (b, p, 0, 0)),
                pl.BlockSpec((None, HEADS_PER_LANE_TILE, n_blocks, V_ROWS, MOBA_BLOCK),
                             lambda b, p, *_: (b, p, 0, 0, 0)),
                pl.BlockSpec((None, None, n_blocks, LANES), lambda b, p, *_: (b, p, 0, 0)),
            ],
            out_specs=pl.BlockSpec((None, S, LANES), lambda b, p, *_: (b, 0, p)),
            scratch_shapes=[
                pltpu.VMEM((HEADS_PER_LANE_TILE, n_slots, MOBA_BLOCK, LANES), BF16),
                pltpu.VMEM((HEADS_PER_LANE_TILE, S, LANES), BF16),
                pltpu.VMEM((HEADS_PER_LANE_TILE, n_slots, SUBLANES, MOBA_BLOCK), F32),
                pltpu.VMEM((HEADS_PER_LANE_TILE, n_slots, STATE_ROWS, MOBA_BLOCK), F32),
            ] + 2 * [pltpu.VMEM((PAIR_GROUP, HEADS_PER_LANE_TILE, MOBA_BLOCK // SUBLANES, SUBLANES, MOBA_BLOCK),
                                F32)]
              + 2 * [pltpu.VMEM((PAIR_GROUP, HEADS_PER_LANE_TILE, SUBLANES, MOBA_BLOCK), F32)],
        ),
        out_shape=jax.ShapeDtypeStruct((B, S, ATTN_WIDTH), BF16),
        compiler_params=cparams(dimension_semantics=("parallel", "parallel")),
        name="moba_attn",
    )(key_tab, query_tab, q, k, vt, kmean)

    x2 = x.reshape(N, D)
    x1 = pl.pallas_call(
        functools.partial(_mix_kernel, tiles_per_seq=tiles_per_seq),
        grid=(N // tm,),
        in_specs=[
            pl.BlockSpec((tm, D), lambda i: (i, 0)),
            pl.BlockSpec((tm, ATTN_WIDTH), lambda i: (i, 0)),
            pl.BlockSpec((tm, POOL_WIDTH), lambda i: (i, 0)),
            pl.BlockSpec((POOL_HALO, POOL_WIDTH), lambda i: (jnp.maximum(i * (tm // POOL_HALO) - 1, 0), 0)),
            _const_spec((1, D)),
            _const_spec(w_gate.shape),
            _const_spec((1, 2 * D)),
            _const_spec(w_pool.shape),
            _const_spec((1, POOL_WIDTH)),
            _const_spec(w_branch_attn.shape),
            _const_spec(w_branch_pool.shape),
            _const_spec(w_out.shape),
        ],
        out_specs=pl.BlockSpec((tm, D), lambda i: (i, 0)),
        out_shape=jax.ShapeDtypeStruct((N, D), F32),
        compiler_params=cparams(dimension_semantics=("parallel",)),
        name="moba_mix",
    )(x2, attn.reshape(N, ATTN_WIDTH), u_pool.reshape(N, POOL_WIDTH), u_pool.reshape(N, POOL_WIDTH),
      norm_mix_g[None, :], w_gate, b_gate[None, :], w_pool.astype(BF16), pool_scale[None, :],
      w_branch_attn.astype(BF16), w_branch_pool.astype(BF16), w_out.astype(BF16))

    n_chunks = d_ff // FF_CHUNK
    tf = min(FFN_TILE, S)
    assert S % tf == 0

    def chunked_cols(w):
        return w.reshape(w.shape[0], n_chunks, FF_CHUNK).transpose(1, 0, 2)

    conv_all = jnp.concatenate([conv_w, conv_b[None, :]], axis=0)
    out = pl.pallas_call(
        functools.partial(_ffn_kernel, tiles_per_seq=S // tf),
        grid=(N // tf,),
        in_specs=[
            pl.BlockSpec((tf, D), lambda i: (i, 0)),
            pl.BlockSpec((CONV_HALO, D), lambda i: (jnp.maximum(i * (tf // CONV_HALO) - 1, 0), 0)),
            _const_spec((1, D)),
            _const_spec((n_chunks, D, FF_CHUNK)),
            _const_spec((n_chunks, D, FF_CHUNK)),
            _const_spec((n_chunks, CONV_WIDTH + 1, FF_CHUNK)),
            _const_spec((n_chunks, CONV_WIDTH + 1, FF_CHUNK)),
            _const_spec((d_ff, D)),
        ],
        out_specs=pl.BlockSpec((tf, D), lambda i: (i, 0)),
        out_shape=jax.ShapeDtypeStruct((N, D), F32),
        scratch_shapes=4 * [pltpu.VMEM((CONV_HALO + tf, FF_CHUNK), F32)]
                       + [pltpu.VMEM((tf, d_ff), BF16)],
        compiler_params=cparams(dimension_semantics=("parallel",)),
        name="moba_ffn",
    )(x1, x1, norm_ffn_g[None, :],
      chunked_cols(w_up[:, :d_ff]).astype(BF16), chunked_cols(w_up[:, d_ff:]).astype(BF16),
      chunked_cols(conv_all[:, :d_ff]), chunked_cols(conv_all[:, d_ff:]),
      w_down.astype(BF16))
    return out.reshape(B, S, D)


def kernel(x, norm_mix_g, w_in, b_gate, q_norm_g, k_norm_g, w_pool, pool_scale, w_branch_attn, w_branch_pool,
           w_out, norm_ffn_g, w_up, conv_w, conv_b, w_down):
    for l in range(w_in.shape[0]):
        x = _layer(x, norm_mix_g[l], w_in[l], b_gate[l], q_norm_g[l], k_norm_g[l], w_pool[l], pool_scale[l],
                   w_branch_attn[l], w_branch_pool[l], w_out[l], norm_ffn_g[l], w_up[l], conv_w[l], conv_b[l],
                   w_down[l])
    return x
```

```python
import functools
import math

import jax
import jax.numpy as jnp
import numpy as np
from jax import lax
from jax.experimental import pallas as pl
from jax.experimental.pallas import tpu as pltpu

ATTN_HEADS = 8
HEAD_DIM = 64
ATTN_WIDTH = ATTN_HEADS * HEAD_DIM
ROPE_DIM = 16
ROPE_HALF = ROPE_DIM // 2
ROPE_THETA = 500000.0
MOBA_BLOCK = 256
MOBA_TOP_K = 3
POOL_WINDOWS = (2, 4, 8, 16)
POOL_WIDTH = 512
POOL_GROUP_DIM = 128
CONV_WIDTH = 3
EPS = 1e-6
NEG_INF = -1e30

LANES = 128
SUBLANES = 8
MXU_DIM = 256
VMEM_LIMIT_BYTES = 56 * 1024 * 1024

HEADS_PER_LANE_TILE = LANES // HEAD_DIM
HEAD_PAIRS = ATTN_HEADS // HEADS_PER_LANE_TILE
POOL_HALO = 16
CONV_HALO = SUBLANES
BF16_SUBLANES = 16
V_ROWS = HEAD_DIM + BF16_SUBLANES
STATE_ROWS = HEAD_DIM + SUBLANES
TOKEN_TILE = 512
FFN_TILE = 512
FF_CHUNK = 256
INIT_UNROLL = 4
DOWN_LAG = 2
PAIR_GROUP = 16

BF16 = jnp.bfloat16
F32 = jnp.float32
_NT = (((1,), (1,)), ((), ()))


def _rms(x):
    return x * lax.rsqrt(jnp.mean(x * x, axis=-1, keepdims=True) + EPS)


def _proj_kernel(x_ref, g1_ref, w_ref, wvt_ref, bd_ref, gq_ref, gk_ref, cos_ref, sa_ref, sb_ref,
                 q_ref, k_ref, vt_ref, up_ref, km_ref, qk0_ref, qk1_ref):
    tm = x_ref.shape[0]
    i = pl.program_id(0)
    n_cols = 2 * ATTN_WIDTH // MXU_DIM

    @pl.when(i == 0)
    def _():
        qk0_ref[...] = jnp.zeros(qk0_ref.shape, F32)

    def step(prev_ref, cur_ref):
        cos, sa, sb = cos_ref[...], sa_ref[...], sb_ref[...]
        for p in range(2 * ATTN_WIDTH // LANES):
            out_ref, po = (q_ref, p) if p < HEAD_PAIRS else (k_ref, p - HEAD_PAIRS)
            tp = prev_ref[:, p * LANES:(p + 1) * LANES]
            piece = tp * cos + pltpu.roll(tp, LANES - ROPE_HALF, 1) * sa + pltpu.roll(tp, ROPE_HALF, 1) * sb
            out_ref[po] = piece.astype(BF16)
            if out_ref is k_ref:
                for blk in range(tm // MOBA_BLOCK):
                    km_ref[blk:blk + 1, po * LANES:(po + 1) * LANES] = jnp.mean(
                        piece[blk * MOBA_BLOCK:(blk + 1) * MOBA_BLOCK], axis=0, keepdims=True)

        h = (_rms(x_ref[...]) * g1_ref[...]).astype(BF16)
        proj = jnp.dot(h, w_ref[...], preferred_element_type=F32)
        up_ref[...] = proj[:, 2 * ATTN_WIDTH:]
        vt = lax.dot_general(wvt_ref[...], h, _NT, preferred_element_type=F32)
        for blk in range(tm // MOBA_BLOCK):
            vt_ref[:, blk] = jnp.concatenate(
                [vt[:, blk * MOBA_BLOCK:(blk + 1) * MOBA_BLOCK].reshape(ATTN_HEADS, HEAD_DIM, MOBA_BLOCK),
                 jnp.ones((ATTN_HEADS, SUBLANES, MOBA_BLOCK), F32),
                 jnp.zeros((ATTN_HEADS, V_ROWS - STATE_ROWS, MOBA_BLOCK), F32)], axis=1).astype(BF16)
        for c in range(n_cols):
            g_ref = gq_ref if c < n_cols // 2 else gk_ref
            col0 = (c % (n_cols // 2)) * MXU_DIM
            t = proj[:, c * MXU_DIM:(c + 1) * MXU_DIM]
            msq = jnp.dot((t * t).astype(BF16), bd_ref[...], preferred_element_type=F32)
            cur_ref[:, c * MXU_DIM:(c + 1) * MXU_DIM] = t * lax.rsqrt(msq + EPS) * g_ref[:, col0:col0 + MXU_DIM]

    for parity, (prev_ref, cur_ref) in enumerate(((qk0_ref, qk1_ref), (qk1_ref, qk0_ref))):
        pl.when(i % 2 == parity)(functools.partial(step, prev_ref, cur_ref))


def _pair_tables(n_blocks):
    remaining = {i: list(range(i)) for i in range(1, n_blocks)}
    padding = [(0, n_blocks + u) for u in range(PAIR_GROUP)]
    groups = []
    while any(remaining.values()):
        live = sorted((i for i in remaining if remaining[i]), key=lambda i: -len(remaining[i]))[:PAIR_GROUP]
        group = [(remaining[i].pop(), i) for i in live]
        groups.append(group + padding[len(group):])
    groups.append(padding)
    flat = [pair for group in groups for pair in group]
    return np.asarray([j for j, _ in flat], np.int32), np.asarray([i for _, i in flat], np.int32)


def _attn_kernel(jt_ref, it_ref, q_ref, k_ref, vt_ref, km_ref, o_ref,
                 qa_ref, ka_ref, m_ref, acc_ref, s0_ref, s1_ref, c0_ref, c1_ref):
    n_blocks = q_ref.shape[0] // MOBA_BLOCK
    n_groups = jt_ref.shape[0] // PAIR_GROUP - 1
    heads = range(HEADS_PER_LANE_TILE)
    lane = lax.broadcasted_iota(jnp.int32, (1, LANES), 1)
    head_masks = [(lane // HEAD_DIM) == a for a in heads]
    spare_lane0 = [(HEADS_PER_LANE_TILE - 1 - a) * HEAD_DIM for a in heads]
    km_heads = [jnp.where(hm, km_ref[...], 0.0).astype(BF16) for hm in head_masks]
    blk_row = lax.broadcasted_iota(jnp.int32, (n_blocks, MOBA_BLOCK), 0)
    key_pos = lax.broadcasted_iota(jnp.int32, (MOBA_BLOCK, MOBA_BLOCK), 0)
    qry_pos = lax.broadcasted_iota(jnp.int32, (MOBA_BLOCK, MOBA_BLOCK), 1)
    causal = key_pos <= qry_pos
    def tiles(s):
        return s.reshape(s.shape[0] // SUBLANES, SUBLANES, s.shape[1])

    def over_keys(reduce_pair, s3):
        r = s3[0]
        for t in range(1, s3.shape[0]):
            r = reduce_pair(r, s3[t])
        for shift in (4, 2, 1):
            r = reduce_pair(r, pltpu.roll(r, shift, 0))
        return r

    def init_block(i):
        r0 = pl.multiple_of(i * MOBA_BLOCK, MOBA_BLOCK)
        q_blk = q_ref[pl.ds(r0, MOBA_BLOCK), :]
        k_own = k_ref[pl.ds(r0, MOBA_BLOCK), :]
        for a in heads:
            ka_ref[a, pl.ds(r0, MOBA_BLOCK), :] = jnp.where(
                head_masks[a], k_own, (lane == spare_lane0[a] + i).astype(BF16))
            qa = jnp.where(head_masks[a], q_blk, jnp.zeros_like(q_blk))
            score = lax.dot_general(km_heads[a], qa, _NT, preferred_element_type=F32)
            score = jnp.where(blk_row < i, score, -jnp.inf)
            chosen = jnp.zeros_like(score)
            for _ in range(MOBA_TOP_K):
                best = jnp.max(score, axis=0, keepdims=True)
                first = jnp.min(jnp.where(score == best, blk_row, n_blocks), axis=0, keepdims=True)
                pick = jnp.logical_and(blk_row == first, best > -jnp.inf)
                chosen = jnp.where(pick, 1.0, chosen)
                score = jnp.where(pick, -jnp.inf, score)
            bias = jnp.where(chosen > 0.0, 0.0, NEG_INF)
            bias_rows = [jnp.zeros((spare_lane0[a], MOBA_BLOCK), F32)] if spare_lane0[a] else []
            bias_rows += [bias, jnp.zeros((LANES - spare_lane0[a] - n_blocks, MOBA_BLOCK), F32)]
            bias_lanes = jnp.concatenate(bias_rows, axis=0).T
            qa_ref[a, i] = jnp.where(head_masks[a], q_blk, bias_lanes.astype(BF16))
            s = lax.dot_general(k_own, qa, _NT, preferred_element_type=F32)
            s3 = tiles(jnp.where(causal, s, NEG_INF))
            m = over_keys(jnp.maximum, s3)
            p3 = jnp.exp2(s3 - m[None])
            m_ref[a, i] = m
            acc_ref[a, i] = jnp.dot(vt_ref[a, i], p3.reshape(MOBA_BLOCK, MOBA_BLOCK).astype(BF16),
                                    preferred_element_type=F32)[:STATE_ROWS]

    def init_blocks(h, carry):
        for d in range(INIT_UNROLL):
            init_block(h * INIT_UNROLL + d)
        return carry

    lax.fori_loop(0, n_blocks // INIT_UNROLL, init_blocks, 0)
    for u in range(PAIR_GROUP):
        for a in heads:
            qa_ref[a, n_blocks + u] = jnp.zeros(qa_ref.shape[2:], qa_ref.dtype)
            m_ref[a, n_blocks + u] = jnp.zeros(m_ref.shape[2:], F32)
            acc_ref[a, n_blocks + u] = jnp.zeros(acc_ref.shape[2:], F32)

    def stage_scores(g, u, a, s_ref, c_ref):
        t = g * PAIR_GROUP + u
        c0 = pl.multiple_of(jt_ref[t] * MOBA_BLOCK, MOBA_BLOCK)
        s3 = tiles(lax.dot_general(ka_ref[a, pl.ds(c0, MOBA_BLOCK), :], qa_ref[a, it_ref[t]], _NT,
                                   preferred_element_type=F32))
        s_ref[u, a] = s3
        c_ref[u, a] = over_keys(jnp.maximum, s3)

    def fold_scores(g, u, a, s_ref, c_ref):
        t = g * PAIR_GROUP + u
        j, i = jt_ref[t], it_ref[t]
        m = m_ref[a, i]
        m_new = jnp.maximum(m, c_ref[u, a])
        m_ref[a, i] = m_new
        alpha = jnp.exp2(m - m_new)
        p3 = jnp.exp2(s_ref[u, a] - m_new[None])
        pv = jnp.dot(vt_ref[a, j], p3.reshape(MOBA_BLOCK, MOBA_BLOCK).astype(BF16),
                     preferred_element_type=F32)[:STATE_ROWS]
        acc_ref[a, i] = (alpha[None] * tiles(acc_ref[a, i])).reshape(pv.shape) + pv

    staging = ((s0_ref, c0_ref), (s1_ref, c1_ref))
    for u in range(PAIR_GROUP):
        for a in heads:
            stage_scores(0, u, a, *staging[0])

    def sweep_group(g, carry):
        for parity in range(2):
            @pl.when(g % 2 == parity)
            def _():
                for u in range(PAIR_GROUP):
                    for a in heads:
                        stage_scores(g + 1, u, a, *staging[1 - parity])
                        fold_scores(g, u, a, *staging[parity])
        return carry

    lax.fori_loop(0, n_groups, sweep_group, 0)

    def finish_block(i, carry):
        r0 = pl.multiple_of(i * MOBA_BLOCK, MOBA_BLOCK)
        out_t = jnp.concatenate(
            [(tiles(acc_ref[a, i, :HEAD_DIM]) / acc_ref[a, i, HEAD_DIM:][None]).reshape(HEAD_DIM, MOBA_BLOCK)
             for a in heads], axis=0)
        o_ref[pl.ds(r0, MOBA_BLOCK), :] = out_t.T.astype(o_ref.dtype)
        return carry

    lax.fori_loop(0, n_blocks, finish_block, 0)


def _mix_kernel(x_ref, attn_ref, u_ref, uh_ref, g1_ref, wg_ref, bg_ref, wp_ref, ps_ref, wba_ref, wbp_ref, wo_ref,
                o_ref, *, tiles_per_seq):
    tm, d = x_ref.shape
    i = pl.program_id(0)
    t_in_seq = (i % tiles_per_seq) * tm
    x = x_ref[...]
    br_attn = jnp.dot(attn_ref[...], wba_ref[...], preferred_element_type=F32)

    halo = jnp.where(t_in_seq == 0, 0.0, uh_ref[...])
    ue = jnp.concatenate([halo, u_ref[...]], axis=0)
    pos = (t_in_seq + lax.broadcasted_iota(jnp.int32, (tm, 1), 0)).astype(F32)
    mixed_pool = []
    for g, w in enumerate(POOL_WINDOWS):
        e = ue[:, g * POOL_GROUP_DIM:(g + 1) * POOL_GROUP_DIM]
        win, span = e, 1
        while span < w:
            win = win + pltpu.roll(win, span, 0)
            span *= 2
        count = jnp.minimum(pos + 1.0, float(w))
        pooled = win[POOL_HALO:] / count - e[POOL_HALO:]
        mixed_pool.append(jnp.dot(pooled.astype(BF16), wp_ref[g], preferred_element_type=F32))
    pooled_mix = jnp.concatenate(mixed_pool, axis=1) * ps_ref[...]

    br_pool = jnp.dot(pooled_mix.astype(BF16), wbp_ref[...], preferred_element_type=F32)
    h = (_rms(x) * g1_ref[...]).astype(BF16)
    gates = jax.nn.sigmoid(jnp.dot(h, wg_ref[...], preferred_element_type=F32) + bg_ref[...])
    mixed = gates[:, :d] * br_attn + gates[:, d:] * br_pool
    o_ref[...] = x + jnp.dot(mixed.astype(BF16), wo_ref[...], preferred_element_type=F32)


def _ffn_kernel(x_ref, xh_ref, g2_ref, wug_ref, wuv_ref, cg_ref, cv_ref, wd_ref, o_ref,
                ug0_ref, uv0_ref, ug1_ref, uv1_ref, act_ref, *, tiles_per_seq):
    tm = x_ref.shape[0]
    n_chunks = wug_ref.shape[0]
    i = pl.program_id(0)
    x = x_ref[...]
    halo = jnp.where(i % tiles_per_seq == 0, 0.0, xh_ref[...])
    h2 = (_rms(jnp.concatenate([halo, x], axis=0)) * g2_ref[...]).astype(BF16)
    staging = [(ug0_ref, uv0_ref), (ug1_ref, uv1_ref)]

    def stage_up(c):
        ug_ref, uv_ref = staging[c % 2]
        ug_ref[...] = jnp.dot(h2, wug_ref[c], preferred_element_type=F32)
        uv_ref[...] = jnp.dot(h2, wuv_ref[c], preferred_element_type=F32)

    def conv(u_ref, cw):
        taps = [u_ref[pl.ds(CONV_HALO - (CONV_WIDTH - 1) + j, tm), :] for j in range(CONV_WIDTH)]
        return taps[0] * cw[0:1] + taps[1] * cw[1:2] + taps[2] * cw[2:3] + cw[3:4]

    def down(c):
        lo, hi = c * FF_CHUNK, (c + 1) * FF_CHUNK
        o_ref[...] += jnp.dot(act_ref[:, lo:hi], wd_ref[lo:hi, :], preferred_element_type=F32)

    o_ref[...] = x
    stage_up(0)
    for c in range(n_chunks):
        if c + 1 < n_chunks:
            stage_up(c + 1)
        if c >= DOWN_LAG:
            down(c - DOWN_LAG)
        ug_ref, uv_ref = staging[c % 2]
        gate = conv(ug_ref, cg_ref[c])
        val = conv(uv_ref, cv_ref[c])
        act_ref[:, c * FF_CHUNK:(c + 1) * FF_CHUNK] = (gate * jax.nn.sigmoid(gate) * val).astype(BF16)
    for c in range(max(n_chunks - DOWN_LAG, 0), n_chunks):
        down(c)


def _const_spec(shape):
    nd = len(shape)
    return pl.BlockSpec(shape, lambda *_: (0,) * nd, pipeline_mode=pl.Buffered(1))


def _rotary_tables(seq):
    inv_freq = ROPE_THETA ** (-jnp.arange(ROPE_HALF, dtype=F32) / ROPE_HALF)
    ang = jnp.arange(seq).astype(F32)[:, None] * inv_freq[None, :]
    cos, sin = jnp.cos(ang), jnp.sin(ang)
    dim = jnp.arange(LANES) % HEAD_DIM
    freq = dim % ROPE_HALF
    cos_t = jnp.where(dim < ROPE_DIM, cos[:, freq], 1.0)
    sin_next = jnp.where(dim < ROPE_HALF, -sin[:, freq], 0.0)
    sin_prev = jnp.where((dim >= ROPE_HALF) & (dim < ROPE_DIM), sin[:, freq], 0.0)
    return cos_t, sin_next, sin_prev


def _layer(x, norm_mix_g, w_in, b_gate, q_norm_g, k_norm_g, w_pool, pool_scale,
           w_branch_attn, w_branch_pool, w_out, norm_ffn_g, w_up, conv_w, conv_b, w_down):
    B, S, D = x.shape
    N = B * S
    d_ff = w_down.shape[0]
    tm = min(TOKEN_TILE, S)
    assert S % tm == 0 and tm % MOBA_BLOCK == 0 and d_ff % FF_CHUNK == 0
    assert S // MOBA_BLOCK <= HEAD_DIM
    assert (S // MOBA_BLOCK) % INIT_UNROLL == 0
    tiles_per_seq = S // tm
    n_blocks = S // MOBA_BLOCK
    blocks_per_tile = tm // MOBA_BLOCK
    cparams = functools.partial(pltpu.CompilerParams, vmem_limit_bytes=VMEM_LIMIT_BYTES)

    w_qkp = jnp.concatenate([w_in[:, :2 * ATTN_WIDTH], w_in[:, 3 * ATTN_WIDTH:3 * ATTN_WIDTH + POOL_WIDTH]],
                            axis=1).astype(BF16)
    w_vt = w_in[:, 2 * ATTN_WIDTH:3 * ATTN_WIDTH].T.astype(BF16)
    w_gate = w_in[:, 3 * ATTN_WIDTH + POOL_WIDTH:].astype(BF16)
    head_of = jnp.arange(MXU_DIM) // HEAD_DIM
    block_diag = jnp.where(head_of[:, None] == head_of[None, :], 1.0 / HEAD_DIM, 0.0).astype(BF16)
    gq = jnp.tile(q_norm_g * (HEAD_DIM ** -0.5 * math.log2(math.e)), ATTN_HEADS)[None, :]
    gk = jnp.tile(k_norm_g, ATTN_HEADS)[None, :]
    cos_t, sin_next, sin_prev = _rotary_tables(S)

    n_tiles = N // tm

    def projected(i):
        j = jnp.minimum(i, n_tiles - 1)
        return j // tiles_per_seq, j % tiles_per_seq

    def finished(i):
        j = jnp.maximum(i - 1, 0)
        return j // tiles_per_seq, j % tiles_per_seq

    table_spec = pl.BlockSpec((tm, LANES), lambda i: (finished(i)[1], 0))
    q, k, vt, u_pool, kmean = pl.pallas_call(
        _proj_kernel,
        grid=(n_tiles + 1,),
        in_specs=[
            pl.BlockSpec((None, tm, D), lambda i: (*projected(i), 0)),
            _const_spec((1, D)),
            _const_spec(w_qkp.shape),
            _const_spec(w_vt.shape),
            _const_spec(block_diag.shape),
            _const_spec(gq.shape),
            _const_spec(gk.shape),
            table_spec, table_spec, table_spec,
        ],
        out_specs=[
            pl.BlockSpec((None, HEAD_PAIRS, tm, LANES), lambda i: (finished(i)[0], 0, finished(i)[1], 0)),
            pl.BlockSpec((None, HEAD_PAIRS, tm, LANES), lambda i: (finished(i)[0], 0, finished(i)[1], 0)),
            pl.BlockSpec((None, ATTN_HEADS, blocks_per_tile, V_ROWS, MOBA_BLOCK),
                         lambda i: (projected(i)[0], 0, projected(i)[1], 0, 0)),
            pl.BlockSpec((None, tm, POOL_WIDTH), lambda i: (*projected(i), 0)),
            pl.BlockSpec((None, None, blocks_per_tile, ATTN_WIDTH), lambda i: (*finished(i), 0, 0)),
        ],
        scratch_shapes=2 * [pltpu.VMEM((tm, 2 * ATTN_WIDTH), F32)],
        out_shape=[
            jax.ShapeDtypeStruct((B, HEAD_PAIRS, S, LANES), BF16),
            jax.ShapeDtypeStruct((B, HEAD_PAIRS, S, LANES), BF16),
            jax.ShapeDtypeStruct((B, ATTN_HEADS, n_blocks, V_ROWS, MOBA_BLOCK), BF16),
            jax.ShapeDtypeStruct((B, S, POOL_WIDTH), F32),
            jax.ShapeDtypeStruct((B, tiles_per_seq, blocks_per_tile, ATTN_WIDTH), F32),
        ],
        compiler_params=cparams(dimension_semantics=("arbitrary",)),
        name="moba_proj",
    )(x, norm_mix_g[None, :], w_qkp, w_vt, block_diag, gq, gk, cos_t, sin_next, sin_prev)

    kmean = kmean.reshape(B, n_blocks, HEAD_PAIRS, LANES).transpose(0, 2, 1, 3)

    key_tab, query_tab = _pair_tables(n_blocks)
    n_slots = n_blocks + PAIR_GROUP
    attn = pl.pallas_call(
        _attn_kernel,
        grid_spec=pltpu.PrefetchScalarGridSpec(
            num_scalar_prefetch=2,
            grid=(B, HEAD_PAIRS),
            in_specs=[
                pl.BlockSpec((None, None, S, LANES), lambda b, p, *_: (b, p, 0, 0)),
                pl.BlockSpec((None, None, S, LANES), lambda b, p, *_: (b, p, 0, 0)),
                pl.BlockSpec((None, HEADS_PER_LANE_TILE, n_blocks, V_ROWS, MOBA_BLOCK),
                             lambda b, p, *_: (b, p, 0, 0, 0)),
                pl.BlockSpec((None, None, n_blocks, LANES), lambda b, p, *_: (b, p, 0, 0)),
            ],
            out_specs=pl.BlockSpec((None, S, LANES), lambda b, p, *_: (b, 0, p)),
            scratch_shapes=[
                pltpu.VMEM((HEADS_PER_LANE_TILE, n_slots, MOBA_BLOCK, LANES), BF16),
                pltpu.VMEM((HEADS_PER_LANE_TILE, S, LANES), BF16),
                pltpu.VMEM((HEADS_PER_LANE_TILE, n_slots, SUBLANES, MOBA_BLOCK), F32),
                pltpu.VMEM((HEADS_PER_LANE_TILE, n_slots, STATE_ROWS, MOBA_BLOCK), F32),
            ] + 2 * [pltpu.VMEM((PAIR_GROUP, HEADS_PER_LANE_TILE, MOBA_BLOCK // SUBLANES, SUBLANES, MOBA_BLOCK),
                                F32)]
              + 2 * [pltpu.VMEM((PAIR_GROUP, HEADS_PER_LANE_TILE, SUBLANES, MOBA_BLOCK), F32)],
        ),
        out_shape=jax.ShapeDtypeStruct((B, S, ATTN_WIDTH), BF16),
        compiler_params=cparams(dimension_semantics=("parallel", "parallel")),
        name="moba_attn",
    )(key_tab, query_tab, q, k, vt, kmean)

    x2 = x.reshape(N, D)
    x1 = pl.pallas_call(
        functools.partial(_mix_kernel, tiles_per_seq=tiles_per_seq),
        grid=(N // tm,),
        in_specs=[
            pl.BlockSpec((tm, D), lambda i: (i, 0)),
            pl.BlockSpec((tm, ATTN_WIDTH), lambda i: (i, 0)),
            pl.BlockSpec((tm, POOL_WIDTH), lambda i: (i, 0)),
            pl.BlockSpec((POOL_HALO, POOL_WIDTH), lambda i: (jnp.maximum(i * (tm // POOL_HALO) - 1, 0), 0)),
            _const_spec((1, D)),
            _const_spec(w_gate.shape),
            _const_spec((1, 2 * D)),
            _const_spec(w_pool.shape),
            _const_spec((1, POOL_WIDTH)),
            _const_spec(w_branch_attn.shape),
            _const_spec(w_branch_pool.shape),
            _const_spec(w_out.shape),
        ],
        out_specs=pl.BlockSpec((tm, D), lambda i: (i, 0)),
        out_shape=jax.ShapeDtypeStruct((N, D), F32),
        compiler_params=cparams(dimension_semantics=("parallel",)),
        name="moba_mix",
    )(x2, attn.reshape(N, ATTN_WIDTH), u_pool.reshape(N, POOL_WIDTH), u_pool.reshape(N, POOL_WIDTH),
      norm_mix_g[None, :], w_gate, b_gate[None, :], w_pool.astype(BF16), pool_scale[None, :],
      w_branch_attn.astype(BF16), w_branch_pool.astype(BF16), w_out.astype(BF16))

    n_chunks = d_ff // FF_CHUNK
    tf = min(FFN_TILE, S)
    assert S % tf == 0

    def chunked_cols(w):
        return w.reshape(w.shape[0], n_chunks, FF_CHUNK).transpose(1, 0, 2)

    conv_all = jnp.concatenate([conv_w, conv_b[None, :]], axis=0)
    out = pl.pallas_call(
        functools.partial(_ffn_kernel, tiles_per_seq=S // tf),
        grid=(N // tf,),
        in_specs=[
            pl.BlockSpec((tf, D), lambda i: (i, 0)),
            pl.BlockSpec((CONV_HALO, D), lambda i: (jnp.maximum(i * (tf // CONV_HALO) - 1, 0), 0)),
            _const_spec((1, D)),
            _const_spec((n_chunks, D, FF_CHUNK)),
            _const_spec((n_chunks, D, FF_CHUNK)),
            _const_spec((n_chunks, CONV_WIDTH + 1, FF_CHUNK)),
            _const_spec((n_chunks, CONV_WIDTH + 1, FF_CHUNK)),
            _const_spec((d_ff, D)),
        ],
        out_specs=pl.BlockSpec((tf, D), lambda i: (i, 0)),
        out_shape=jax.ShapeDtypeStruct((N, D), F32),
        scratch_shapes=4 * [pltpu.VMEM((CONV_HALO + tf, FF_CHUNK), F32)]
                       + [pltpu.VMEM((tf, d_ff), BF16)],
        compiler_params=cparams(dimension_semantics=("parallel",)),
        name="moba_ffn",
    )(x1, x1, norm_ffn_g[None, :],
      chunked_cols(w_up[:, :d_ff]).astype(BF16), chunked_cols(w_up[:, d_ff:]).astype(BF16),
      chunked_cols(conv_all[:, :d_ff]), chunked_cols(conv_all[:, d_ff:]),
      w_down.astype(BF16))
    return out.reshape(B, S, D)


def kernel(x, norm_mix_g, w_in, b_gate, q_norm_g, k_norm_g, w_pool, pool_scale, w_branch_attn, w_branch_pool,
           w_out, norm_ffn_g, w_up, conv_w, conv_b, w_down):
    for l in range(w_in.shape[0]):
        x = _layer(x, norm_mix_g[l], w_in[l], b_gate[l], q_norm_g[l], k_norm_g[l], w_pool[l], pool_scale[l],
                   w_branch_attn[l], w_branch_pool[l], w_out[l], norm_ffn_g[l], w_up[l], conv_w[l], conv_b[l],
                   w_down[l])
    return x
```

```python
import functools
import math

import jax
import jax.numpy as jnp
import numpy as np
from jax import lax
from jax.experimental import pallas as pl
from jax.experimental.pallas import tpu as pltpu

ATTN_HEADS = 8
HEAD_DIM = 64
ATTN_WIDTH = ATTN_HEADS * HEAD_DIM
ROPE_DIM = 16
ROPE_HALF = ROPE_DIM // 2
ROPE_THETA = 500000.0
MOBA_BLOCK = 256
MOBA_TOP_K = 3
POOL_WINDOWS = (2, 4, 8, 16)
POOL_WIDTH = 512
POOL_GROUP_DIM = 128
CONV_WIDTH = 3
EPS = 1e-6
NEG_INF = -1e30

LANES = 128
SUBLANES = 8
MXU_DIM = 256
VMEM_LIMIT_BYTES = 56 * 1024 * 1024

HEADS_PER_LANE_TILE = LANES // HEAD_DIM
HEAD_PAIRS = ATTN_HEADS // HEADS_PER_LANE_TILE
POOL_HALO = 16
CONV_HALO = SUBLANES
BF16_SUBLANES = 16
V_ROWS = HEAD_DIM + BF16_SUBLANES
STATE_ROWS = HEAD_DIM + SUBLANES
TOKEN_TILE = 512
FFN_TILE = 512
FF_CHUNK = 256
INIT_UNROLL = 4
DOWN_LAG = 2
PAIR_GROUP = 16

BF16 = jnp.bfloat16
F32 = jnp.float32
_NT = (((1,), (1,)), ((), ()))


def _rms(x):
    return x * lax.rsqrt(jnp.mean(x * x, axis=-1, keepdims=True) + EPS)


def _proj_kernel(x_ref, g1_ref, w_ref, wvt_ref, bd_ref, gq_ref, gk_ref, cos_ref, sa_ref, sb_ref,
                 q_ref, k_ref, vt_ref, up_ref, km_ref, qk0_ref, qk1_ref):
    tm = x_ref.shape[0]
    i = pl.program_id(0)
    n_cols = 2 * ATTN_WIDTH // MXU_DIM

    @pl.when(i == 0)
    def _():
        qk0_ref[...] = jnp.zeros(qk0_ref.shape, F32)

    def step(prev_ref, cur_ref):
        cos, sa, sb = cos_ref[...], sa_ref[...], sb_ref[...]
        for p in range(2 * ATTN_WIDTH // LANES):
            out_ref, po = (q_ref, p) if p < HEAD_PAIRS else (k_ref, p - HEAD_PAIRS)
            tp = prev_ref[:, p * LANES:(p + 1) * LANES]
            piece = tp * cos + pltpu.roll(tp, LANES - ROPE_HALF, 1) * sa + pltpu.roll(tp, ROPE_HALF, 1) * sb
            out_ref[po] = piece.astype(BF16)
            if out_ref is k_ref:
                for blk in range(tm // MOBA_BLOCK):
                    km_ref[blk:blk + 1, po * LANES:(po + 1) * LANES] = jnp.mean(
                        piece[blk * MOBA_BLOCK:(blk + 1) * MOBA_BLOCK], axis=0, keepdims=True)

        h = (_rms(x_ref[...]) * g1_ref[...]).astype(BF16)
        proj = jnp.dot(h, w_ref[...], preferred_element_type=F32)
        up_ref[...] = proj[:, 2 * ATTN_WIDTH:]
        vt = lax.dot_general(wvt_ref[...], h, _NT, preferred_element_type=F32)
        for blk in range(tm // MOBA_BLOCK):
            vt_ref[:, blk] = jnp.concatenate(
                [vt[:, blk * MOBA_BLOCK:(blk + 1) * MOBA_BLOCK].reshape(ATTN_HEADS, HEAD_DIM, MOBA_BLOCK),
                 jnp.ones((ATTN_HEADS, SUBLANES, MOBA_BLOCK), F32),
                 jnp.zeros((ATTN_HEADS, V_ROWS - STATE_ROWS, MOBA_BLOCK), F32)], axis=1).astype(BF16)
        for c in range(n_cols):
            g_ref = gq_ref if c < n_cols // 2 else gk_ref
            col0 = (c % (n_cols // 2)) * MXU_DIM
            t = proj[:, c * MXU_DIM:(c + 1) * MXU_DIM]
            msq = jnp.dot((t * t).astype(BF16), bd_ref[...], preferred_element_type=F32)
            cur_ref[:, c * MXU_DIM:(c + 1) * MXU_DIM] = t * lax.rsqrt(msq + EPS) * g_ref[:, col0:col0 + MXU_DIM]

    for parity, (prev_ref, cur_ref) in enumerate(((qk0_ref, qk1_ref), (qk1_ref, qk0_ref))):
        pl.when(i % 2 == parity)(functools.partial(step, prev_ref, cur_ref))


def _pair_tables(n_blocks):
    remaining = {i: list(range(i)) for i in range(1, n_blocks)}
    padding = [(0, n_blocks + u) for u in range(PAIR_GROUP)]
    groups = []
    while any(remaining.values()):
        live = sorted((i for i in remaining if remaining[i]), key=lambda i: -len(remaining[i]))[:PAIR_GROUP]
        group = [(remaining[i].pop(), i) for i in live]
        groups.append(group + padding[len(group):])
    groups.append(padding)
    flat = [pair for group in groups for pair in group]
    return np.asarray([j for j, _ in flat], np.int32), np.asarray([i for _, i in flat], np.int32)


def _attn_kernel(jt_ref, it_ref, q_ref, k_ref, vt_ref, km_ref, o_ref,
                 qa_ref, ka_ref, m_ref, acc_ref, s0_ref, s1_ref, c0_ref, c1_ref):
    n_blocks = q_ref.shape[0] // MOBA_BLOCK
    n_groups = jt_ref.shape[0] // PAIR_GROUP - 1
    heads = range(HEADS_PER_LANE_TILE)
    lane = lax.broadcasted_iota(jnp.int32, (1, LANES), 1)
    head_masks = [(lane // HEAD_DIM) == a for a in heads]
    spare_lane0 = [(HEADS_PER_LANE_TILE - 1 - a) * HEAD_DIM for a in heads]
    km_heads = [jnp.where(hm, km_ref[...], 0.0).astype(BF16) for hm in head_masks]
    blk_row = lax.broadcasted_iota(jnp.int32, (n_blocks, MOBA_BLOCK), 0)
    key_pos = lax.broadcasted_iota(jnp.int32, (MOBA_BLOCK, MOBA_BLOCK), 0)
    qry_pos = lax.broadcasted_iota(jnp.int32, (MOBA_BLOCK, MOBA_BLOCK), 1)
    causal = key_pos <= qry_pos
    def tiles(s):
        return s.reshape(s.shape[0] // SUBLANES, SUBLANES, s.shape[1])

    def over_keys(reduce_pair, s3):
        r = s3[0]
        for t in range(1, s3.shape[0]):
            r = reduce_pair(r, s3[t])
        for shift in (4, 2, 1):
            r = reduce_pair(r, pltpu.roll(r, shift, 0))
        return r

    def init_block(i):
        r0 = pl.multiple_of(i * MOBA_BLOCK, MOBA_BLOCK)
        q_blk = q_ref[pl.ds(r0, MOBA_BLOCK), :]
        k_own = k_ref[pl.ds(r0, MOBA_BLOCK), :]
        for a in heads:
            ka_ref[a, pl.ds(r0, MOBA_BLOCK), :] = jnp.where(
                head_masks[a], k_own, (lane == spare_lane0[a] + i).astype(BF16))
            qa = jnp.where(head_masks[a], q_blk, jnp.zeros_like(q_blk))
            score = lax.dot_general(km_heads[a], qa, _NT, preferred_element_type=F32)
            score = jnp.where(blk_row < i, score, -jnp.inf)
            chosen = jnp.zeros_like(score)
            for _ in range(MOBA_TOP_K):
                best = jnp.max(score, axis=0, keepdims=True)
                first = jnp.min(jnp.where(score == best, blk_row, n_blocks), axis=0, keepdims=True)
                pick = jnp.logical_and(blk_row == first, best > -jnp.inf)
                chosen = jnp.where(pick, 1.0, chosen)
                score = jnp.where(pick, -jnp.inf, score)
            bias = jnp.where(chosen > 0.0, 0.0, NEG_INF)
            bias_rows = [jnp.zeros((spare_lane0[a], MOBA_BLOCK), F32)] if spare_lane0[a] else []
            bias_rows += [bias, jnp.zeros((LANES - spare_lane0[a] - n_blocks, MOBA_BLOCK), F32)]
            bias_lanes = jnp.concatenate(bias_rows, axis=0).T
            qa_ref[a, i] = jnp.where(head_masks[a], q_blk, bias_lanes.astype(BF16))
            s = lax.dot_general(k_own, qa, _NT, preferred_element_type=F32)
            s3 = tiles(jnp.where(causal, s, NEG_INF))
            m = over_keys(jnp.maximum, s3)
            p3 = jnp.exp2(s3 - m[None])
            m_ref[a, i] = m
            acc_ref[a, i] = jnp.dot(vt_ref[a, i], p3.reshape(MOBA_BLOCK, MOBA_BLOCK).astype(BF16),
                                    preferred_element_type=F32)[:STATE_ROWS]

    def init_blocks(h, carry):
        for d in range(INIT_UNROLL):
            init_block(h * INIT_UNROLL + d)
        return carry

    lax.fori_loop(0, n_blocks // INIT_UNROLL, init_blocks, 0)
    for u in range(PAIR_GROUP):
        for a in heads:
            qa_ref[a, n_blocks + u] = jnp.zeros(qa_ref.shape[2:], qa_ref.dtype)
            m_ref[a, n_blocks + u] = jnp.zeros(m_ref.shape[2:], F32)
            acc_ref[a, n_blocks + u] = jnp.zeros(acc_ref.shape[2:], F32)

    def stage_scores(g, u, a, s_ref, c_ref):
        t = g * PAIR_GROUP + u
        c0 = pl.multiple_of(jt_ref[t] * MOBA_BLOCK, MOBA_BLOCK)
        s3 = tiles(lax.dot_general(ka_ref[a, pl.ds(c0, MOBA_BLOCK), :], qa_ref[a, it_ref[t]], _NT,
                                   preferred_element_type=F32))
        s_ref[u, a] = s3
        c_ref[u, a] = over_keys(jnp.maximum, s3)

    def fold_scores(g, u, a, s_ref, c_ref):
        t = g * PAIR_GROUP + u
        j, i = jt_ref[t], it_ref[t]
        m = m_ref[a, i]
        m_new = jnp.maximum(m, c_ref[u, a])
        m_ref[a, i] = m_new
        alpha = jnp.exp2(m - m_new)
        p3 = jnp.exp2(s_ref[u, a] - m_new[None])
        pv = jnp.dot(vt_ref[a, j], p3.reshape(MOBA_BLOCK, MOBA_BLOCK).astype(BF16),
                     preferred_element_type=F32)[:STATE_ROWS]
        acc_ref[a, i] = (alpha[None] * tiles(acc_ref[a, i])).reshape(pv.shape) + pv

    staging = ((s0_ref, c0_ref), (s1_ref, c1_ref))
    for u in range(PAIR_GROUP):
        for a in heads:
            stage_scores(0, u, a, *staging[0])

    def sweep_group(g, carry):
        for parity in range(2):
            @pl.when(g % 2 == parity)
            def _():
                for u in range(PAIR_GROUP):
                    for a in heads:
                        stage_scores(g + 1, u, a, *staging[1 - parity])
                        fold_scores(g, u, a, *staging[parity])
        return carry

    lax.fori_loop(0, n_groups, sweep_group, 0)

    def finish_blocks(h, carry):
        for d in range(INIT_UNROLL):
            i = h * INIT_UNROLL + d
            r0 = pl.multiple_of(i * MOBA_BLOCK, MOBA_BLOCK)
            out_t = jnp.concatenate(
                [(tiles(acc_ref[a, i, :HEAD_DIM]) / acc_ref[a, i, HEAD_DIM:][None]).reshape(HEAD_DIM, MOBA_BLOCK)
                 for a in heads], axis=0)
            o_ref[pl.ds(r0, MOBA_BLOCK), :] = out_t.T.astype(o_ref.dtype)
        return carry

    lax.fori_loop(0, n_blocks // INIT_UNROLL, finish_blocks, 0)


def _mix_kernel(x_ref, attn_ref, u_ref, uh_ref, g1_ref, wg_ref, bg_ref, wp_ref, ps_ref, wba_ref, wbp_ref, wo_ref,
                o_ref, *, tiles_per_seq):
    tm, d = x_ref.shape
    i = pl.program_id(0)
    t_in_seq = (i % tiles_per_seq) * tm
    x = x_ref[...]
    h = (_rms(x) * g1_ref[...]).astype(BF16)
    gates = jax.nn.sigmoid(jnp.dot(h, wg_ref[...], preferred_element_type=F32) + bg_ref[...])

    halo = jnp.where(t_in_seq == 0, 0.0, uh_ref[...])
    ue = jnp.concatenate([halo, u_ref[...]], axis=0)
    pos = (t_in_seq + lax.broadcasted_iota(jnp.int32, (tm, 1), 0)).astype(F32)
    mixed_pool = []
    for g, w in enumerate(POOL_WINDOWS):
        e = ue[:, g * POOL_GROUP_DIM:(g + 1) * POOL_GROUP_DIM]
        win, span = e, 1
        while span < w:
            win = win + pltpu.roll(win, span, 0)
            span *= 2
        count = jnp.minimum(pos + 1.0, float(w))
        pooled = win[POOL_HALO:] / count - e[POOL_HALO:]
        mixed_pool.append(jnp.dot(pooled.astype(BF16), wp_ref[g], preferred_element_type=F32))
    pooled_mix = jnp.concatenate(mixed_pool, axis=1) * ps_ref[...]

    br_attn = jnp.dot(attn_ref[...], wba_ref[...], preferred_element_type=F32)
    br_pool = jnp.dot(pooled_mix.astype(BF16), wbp_ref[...], preferred_element_type=F32)
    mixed = gates[:, :d] * br_attn + gates[:, d:] * br_pool
    o_ref[...] = x + jnp.dot(mixed.astype(BF16), wo_ref[...], preferred_element_type=F32)


def _ffn_kernel(x_ref, xh_ref, g2_ref, wug_ref, wuv_ref, cg_ref, cv_ref, wd_ref, o_ref,
                ug0_ref, uv0_ref, ug1_ref, uv1_ref, act_ref, *, tiles_per_seq):
    tm = x_ref.shape[0]
    n_chunks = wug_ref.shape[0]
    i = pl.program_id(0)
    x = x_ref[...]
    halo = jnp.where(i % tiles_per_seq == 0, 0.0, xh_ref[...])
    h2 = (_rms(jnp.concatenate([halo, x], axis=0)) * g2_ref[...]).astype(BF16)
    staging = [(ug0_ref, uv0_ref), (ug1_ref, uv1_ref)]

    def stage_up(c):
        ug_ref, uv_ref = staging[c % 2]
        ug_ref[...] = jnp.dot(h2, wug_ref[c], preferred_element_type=F32)
        uv_ref[...] = jnp.dot(h2, wuv_ref[c], preferred_element_type=F32)

    def conv(u_ref, cw):
        taps = [u_ref[pl.ds(CONV_HALO - (CONV_WIDTH - 1) + j, tm), :] for j in range(CONV_WIDTH)]
        return taps[0] * cw[0:1] + taps[1] * cw[1:2] + taps[2] * cw[2:3] + cw[3:4]

    def down(c):
        lo, hi = c * FF_CHUNK, (c + 1) * FF_CHUNK
        o_ref[...] += jnp.dot(act_ref[:, lo:hi], wd_ref[lo:hi, :], preferred_element_type=F32)

    o_ref[...] = x
    stage_up(0)
    for c in range(n_chunks):
        if c + 1 < n_chunks:
            stage_up(c + 1)
        if c >= DOWN_LAG:
            down(c - DOWN_LAG)
        ug_ref, uv_ref = staging[c % 2]
        gate = conv(ug_ref, cg_ref[c])
        val = conv(uv_ref, cv_ref[c])
        act_ref[:, c * FF_CHUNK:(c + 1) * FF_CHUNK] = (gate * jax.nn.sigmoid(gate) * val).astype(BF16)
    for c in range(max(n_chunks - DOWN_LAG, 0), n_chunks):
        down(c)


def _const_spec(shape):
    nd = len(shape)
    return pl.BlockSpec(shape, lambda *_: (0,) * nd, pipeline_mode=pl.Buffered(1))


def _rotary_tables(seq):
    inv_freq = ROPE_THETA ** (-jnp.arange(ROPE_HALF, dtype=F32) / ROPE_HALF)
    ang = jnp.arange(seq).astype(F32)[:, None] * inv_freq[None, :]
    cos, sin = jnp.cos(ang), jnp.sin(ang)
    dim = jnp.arange(LANES) % HEAD_DIM
    freq = dim % ROPE_HALF
    cos_t = jnp.where(dim < ROPE_DIM, cos[:, freq], 1.0)
    sin_next = jnp.where(dim < ROPE_HALF, -sin[:, freq], 0.0)
    sin_prev = jnp.where((dim >= ROPE_HALF) & (dim < ROPE_DIM), sin[:, freq], 0.0)
    return cos_t, sin_next, sin_prev


def _layer(x, norm_mix_g, w_in, b_gate, q_norm_g, k_norm_g, w_pool, pool_scale,
           w_branch_attn, w_branch_pool, w_out, norm_ffn_g, w_up, conv_w, conv_b, w_down):
    B, S, D = x.shape
    N = B * S
    d_ff = w_down.shape[0]
    tm = min(TOKEN_TILE, S)
    assert S % tm == 0 and tm % MOBA_BLOCK == 0 and d_ff % FF_CHUNK == 0
    assert S // MOBA_BLOCK <= HEAD_DIM
    assert (S // MOBA_BLOCK) % INIT_UNROLL == 0
    tiles_per_seq = S // tm
    n_blocks = S // MOBA_BLOCK
    blocks_per_tile = tm // MOBA_BLOCK
    cparams = functools.partial(pltpu.CompilerParams, vmem_limit_bytes=VMEM_LIMIT_BYTES)

    w_qkp = jnp.concatenate([w_in[:, :2 * ATTN_WIDTH], w_in[:, 3 * ATTN_WIDTH:3 * ATTN_WIDTH + POOL_WIDTH]],
                            axis=1).astype(BF16)
    w_vt = w_in[:, 2 * ATTN_WIDTH:3 * ATTN_WIDTH].T.astype(BF16)
    w_gate = w_in[:, 3 * ATTN_WIDTH + POOL_WIDTH:].astype(BF16)
    head_of = jnp.arange(MXU_DIM) // HEAD_DIM
    block_diag = jnp.where(head_of[:, None] == head_of[None, :], 1.0 / HEAD_DIM, 0.0).astype(BF16)
    gq = jnp.tile(q_norm_g * (HEAD_DIM ** -0.5 * math.log2(math.e)), ATTN_HEADS)[None, :]
    gk = jnp.tile(k_norm_g, ATTN_HEADS)[None, :]
    cos_t, sin_next, sin_prev = _rotary_tables(S)

    n_tiles = N // tm

    def projected(i):
        j = jnp.minimum(i, n_tiles - 1)
        return j // tiles_per_seq, j % tiles_per_seq

    def finished(i):
        j = jnp.maximum(i - 1, 0)
        return j // tiles_per_seq, j % tiles_per_seq

    table_spec = pl.BlockSpec((tm, LANES), lambda i: (finished(i)[1], 0))
    q, k, vt, u_pool, kmean = pl.pallas_call(
        _proj_kernel,
        grid=(n_tiles + 1,),
        in_specs=[
            pl.BlockSpec((None, tm, D), lambda i: (*projected(i), 0)),
            _const_spec((1, D)),
            _const_spec(w_qkp.shape),
            _const_spec(w_vt.shape),
            _const_spec(block_diag.shape),
            _const_spec(gq.shape),
            _const_spec(gk.shape),
            table_spec, table_spec, table_spec,
        ],
        out_specs=[
            pl.BlockSpec((None, HEAD_PAIRS, tm, LANES), lambda i: (finished(i)[0], 0, finished(i)[1], 0)),
            pl.BlockSpec((None, HEAD_PAIRS, tm, LANES), lambda i: (finished(i)[0], 0, finished(i)[1], 0)),
            pl.BlockSpec((None, ATTN_HEADS, blocks_per_tile, V_ROWS, MOBA_BLOCK),
                         lambda i: (projected(i)[0], 0, projected(i)[1], 0, 0)),
            pl.BlockSpec((None, tm, POOL_WIDTH), lambda i: (*projected(i), 0)),
            pl.BlockSpec((None, None, blocks_per_tile, ATTN_WIDTH), lambda i: (*finished(i), 0, 0)),
        ],
        scratch_shapes=2 * [pltpu.VMEM((tm, 2 * ATTN_WIDTH), F32)],
        out_shape=[
            jax.ShapeDtypeStruct((B, HEAD_PAIRS, S, LANES), BF16),
            jax.ShapeDtypeStruct((B, HEAD_PAIRS, S, LANES), BF16),
            jax.ShapeDtypeStruct((B, ATTN_HEADS, n_blocks, V_ROWS, MOBA_BLOCK), BF16),
            jax.ShapeDtypeStruct((B, S, POOL_WIDTH), F32),
            jax.ShapeDtypeStruct((B, tiles_per_seq, blocks_per_tile, ATTN_WIDTH), F32),
        ],
        compiler_params=cparams(dimension_semantics=("arbitrary",)),
        name="moba_proj",
    )(x, norm_mix_g[None, :], w_qkp, w_vt, block_diag, gq, gk, cos_t, sin_next, sin_prev)

    kmean = kmean.reshape(B, n_blocks, HEAD_PAIRS, LANES).transpose(0, 2, 1, 3)

    key_tab, query_tab = _pair_tables(n_blocks)
    n_slots = n_blocks + PAIR_GROUP
    attn = pl.pallas_call(
        _attn_kernel,
        grid_spec=pltpu.PrefetchScalarGridSpec(
            num_scalar_prefetch=2,
            grid=(B, HEAD_PAIRS),
            in_specs=[
                pl.BlockSpec((None, None, S, LANES), lambda b, p, *_: (b, p, 0, 0)),
                pl.BlockSpec((None, None, S, LANES), lambda b, p, *_: (b, p, 0, 0)),
                pl.BlockSpec((None, HEADS_PER_LANE_TILE, n_blocks, V_ROWS, MOBA_BLOCK),
                             lambda b, p, *_: (b, p, 0, 0, 0)),
                pl.BlockSpec((None, None, n_blocks, LANES), lambda b, p, *_: (b, p, 0, 0)),
            ],
            out_specs=pl.BlockSpec((None, S, LANES), lambda b, p, *_: (b, 0, p)),
            scratch_shapes=[
                pltpu.VMEM((HEADS_PER_LANE_TILE, n_slots, MOBA_BLOCK, LANES), BF16),
                pltpu.VMEM((HEADS_PER_LANE_TILE, S, LANES), BF16),
                pltpu.VMEM((HEADS_PER_LANE_TILE, n_slots, SUBLANES, MOBA_BLOCK), F32),
                pltpu.VMEM((HEADS_PER_LANE_TILE, n_slots, STATE_ROWS, MOBA_BLOCK), F32),
            ] + 2 * [pltpu.VMEM((PAIR_GROUP, HEADS_PER_LANE_TILE, MOBA_BLOCK // SUBLANES, SUBLANES, MOBA_BLOCK),
                                F32)]
              + 2 * [pltpu.VMEM((PAIR_GROUP, HEADS_PER_LANE_TILE, SUBLANES, MOBA_BLOCK), F32)],
        ),
        out_shape=jax.ShapeDtypeStruct((B, S, ATTN_WIDTH), BF16),
        compiler_params=cparams(dimension_semantics=("parallel", "parallel")),
        name="moba_attn",
    )(key_tab, query_tab, q, k, vt, kmean)

    x2 = x.reshape(N, D)
    x1 = pl.pallas_call(
        functools.partial(_mix_kernel, tiles_per_seq=tiles_per_seq),
        grid=(N // tm,),
        in_specs=[
            pl.BlockSpec((tm, D), lambda i: (i, 0)),
            pl.BlockSpec((tm, ATTN_WIDTH), lambda i: (i, 0)),
            pl.BlockSpec((tm, POOL_WIDTH), lambda i: (i, 0)),
            pl.BlockSpec((POOL_HALO, POOL_WIDTH), lambda i: (jnp.maximum(i * (tm // POOL_HALO) - 1, 0), 0)),
            _const_spec((1, D)),
            _const_spec(w_gate.shape),
            _const_spec((1, 2 * D)),
            _const_spec(w_pool.shape),
            _const_spec((1, POOL_WIDTH)),
            _const_spec(w_branch_attn.shape),
            _const_spec(w_branch_pool.shape),
            _const_spec(w_out.shape),
        ],
        out_specs=pl.BlockSpec((tm, D), lambda i: (i, 0)),
        out_shape=jax.ShapeDtypeStruct((N, D), F32),
        compiler_params=cparams(dimension_semantics=("parallel",)),
        name="moba_mix",
    )(x2, attn.reshape(N, ATTN_WIDTH), u_pool.reshape(N, POOL_WIDTH), u_pool.reshape(N, POOL_WIDTH),
      norm_mix_g[None, :], w_gate, b_gate[None, :], w_pool.astype(BF16), pool_scale[None, :],
      w_branch_attn.astype(BF16), w_branch_pool.astype(BF16), w_out.astype(BF16))

    n_chunks = d_ff // FF_CHUNK
    tf = min(FFN_TILE, S)
    assert S % tf == 0

    def chunked_cols(w):
        return w.reshape(w.shape[0], n_chunks, FF_CHUNK).transpose(1, 0, 2)

    conv_all = jnp.concatenate([conv_w, conv_b[None, :]], axis=0)
    out = pl.pallas_call(
        functools.partial(_ffn_kernel, tiles_per_seq=S // tf),
        grid=(N // tf,),
        in_specs=[
            pl.BlockSpec((tf, D), lambda i: (i, 0)),
            pl.BlockSpec((CONV_HALO, D), lambda i: (jnp.maximum(i * (tf // CONV_HALO) - 1, 0), 0)),
            _const_spec((1, D)),
            _const_spec((n_chunks, D, FF_CHUNK)),
            _const_spec((n_chunks, D, FF_CHUNK)),
            _const_spec((n_chunks, CONV_WIDTH + 1, FF_CHUNK)),
            _const_spec((n_chunks, CONV_WIDTH + 1, FF_CHUNK)),
            _const_spec((d_ff, D)),
        ],
        out_specs=pl.BlockSpec((tf, D), lambda i: (i, 0)),
        out_shape=jax.ShapeDtypeStruct((N, D), F32),
        scratch_shapes=4 * [pltpu.VMEM((CONV_HALO + tf, FF_CHUNK), F32)]
                       + [pltpu.VMEM((tf, d_ff), BF16)],
        compiler_params=cparams(dimension_semantics=("parallel",)),
        name="moba_ffn",
    )(x1, x1, norm_ffn_g[None, :],
      chunked_cols(w_up[:, :d_ff]).astype(BF16), chunked_cols(w_up[:, d_ff:]).astype(BF16),
      chunked_cols(conv_all[:, :d_ff]), chunked_cols(conv_all[:, d_ff:]),
      w_down.astype(BF16))
    return out.reshape(B, S, D)


def kernel(x, norm_mix_g, w_in, b_gate, q_norm_g, k_norm_g, w_pool, pool_scale, w_branch_attn, w_branch_pool,
           w_out, norm_ffn_g, w_up, conv_w, conv_b, w_down):
    for l in range(w_in.shape[0]):
        x = _layer(x, norm_mix_g[l], w_in[l], b_gate[l], q_norm_g[l], k_norm_g[l], w_pool[l], pool_scale[l],
                   w_branch_attn[l], w_branch_pool[l], w_out[l], norm_ffn_g[l], w_up[l], conv_w[l], conv_b[l],
                   w_down[l])
    return x
```

```python
import functools
import math

import jax
import jax.numpy as jnp
import numpy as np
from jax import lax
from jax.experimental import pallas as pl
from jax.experimental.pallas import tpu as pltpu

ATTN_HEADS = 8
HEAD_DIM = 64
ATTN_WIDTH = ATTN_HEADS * HEAD_DIM
ROPE_DIM = 16
ROPE_HALF = ROPE_DIM // 2
ROPE_THETA = 500000.0
MOBA_BLOCK = 256
MOBA_TOP_K = 3
POOL_WINDOWS = (2, 4, 8, 16)
POOL_WIDTH = 512
POOL_GROUP_DIM = 128
CONV_WIDTH = 3
EPS = 1e-6
NEG_INF = -1e30

LANES = 128
SUBLANES = 8
MXU_DIM = 256
VMEM_LIMIT_BYTES = 56 * 1024 * 1024

HEADS_PER_LANE_TILE = LANES // HEAD_DIM
HEAD_PAIRS = ATTN_HEADS // HEADS_PER_LANE_TILE
POOL_HALO = 16
CONV_HALO = SUBLANES
BF16_SUBLANES = 16
V_ROWS = HEAD_DIM + BF16_SUBLANES
STATE_ROWS = HEAD_DIM + SUBLANES
TOKEN_TILE = 512
FFN_TILE = 512
FF_CHUNK = 256
INIT_UNROLL = 4
DOWN_LAG = 3
PAIR_GROUP = 16

BF16 = jnp.bfloat16
F32 = jnp.float32
_NT = (((1,), (1,)), ((), ()))


def _rms(x):
    return x * lax.rsqrt(jnp.mean(x * x, axis=-1, keepdims=True) + EPS)


def _proj_kernel(x_ref, g1_ref, w_ref, wvt_ref, bd_ref, gq_ref, gk_ref, cos_ref, sa_ref, sb_ref,
                 q_ref, k_ref, vt_ref, up_ref, km_ref, qk0_ref, qk1_ref):
    tm = x_ref.shape[0]
    i = pl.program_id(0)
    n_cols = 2 * ATTN_WIDTH // MXU_DIM

    @pl.when(i == 0)
    def _():
        qk0_ref[...] = jnp.zeros(qk0_ref.shape, F32)

    def step(prev_ref, cur_ref):
        cos, sa, sb = cos_ref[...], sa_ref[...], sb_ref[...]
        for p in range(2 * ATTN_WIDTH // LANES):
            out_ref, po = (q_ref, p) if p < HEAD_PAIRS else (k_ref, p - HEAD_PAIRS)
            tp = prev_ref[:, p * LANES:(p + 1) * LANES]
            piece = tp * cos + pltpu.roll(tp, LANES - ROPE_HALF, 1) * sa + pltpu.roll(tp, ROPE_HALF, 1) * sb
            out_ref[po] = piece.astype(BF16)
            if out_ref is k_ref:
                for blk in range(tm // MOBA_BLOCK):
                    km_ref[blk:blk + 1, po * LANES:(po + 1) * LANES] = jnp.mean(
                        piece[blk * MOBA_BLOCK:(blk + 1) * MOBA_BLOCK], axis=0, keepdims=True)

        h = (_rms(x_ref[...]) * g1_ref[...]).astype(BF16)
        proj = jnp.dot(h, w_ref[...], preferred_element_type=F32)
        up_ref[...] = proj[:, 2 * ATTN_WIDTH:]
        vt = lax.dot_general(wvt_ref[...], h, _NT, preferred_element_type=F32)
        for blk in range(tm // MOBA_BLOCK):
            vt_ref[:, blk] = jnp.concatenate(
                [vt[:, blk * MOBA_BLOCK:(blk + 1) * MOBA_BLOCK].reshape(ATTN_HEADS, HEAD_DIM, MOBA_BLOCK),
                 jnp.ones((ATTN_HEADS, SUBLANES, MOBA_BLOCK), F32),
                 jnp.zeros((ATTN_HEADS, V_ROWS - STATE_ROWS, MOBA_BLOCK), F32)], axis=1).astype(BF16)
        for c in range(n_cols):
            g_ref = gq_ref if c < n_cols // 2 else gk_ref
            col0 = (c % (n_cols // 2)) * MXU_DIM
            t = proj[:, c * MXU_DIM:(c + 1) * MXU_DIM]
            msq = jnp.dot((t * t).astype(BF16), bd_ref[...], preferred_element_type=F32)
            cur_ref[:, c * MXU_DIM:(c + 1) * MXU_DIM] = t * lax.rsqrt(msq + EPS) * g_ref[:, col0:col0 + MXU_DIM]

    for parity, (prev_ref, cur_ref) in enumerate(((qk0_ref, qk1_ref), (qk1_ref, qk0_ref))):
        pl.when(i % 2 == parity)(functools.partial(step, prev_ref, cur_ref))


def _pair_tables(n_blocks):
    remaining = {i: list(range(i)) for i in range(1, n_blocks)}
    padding = [(0, n_blocks + u) for u in range(PAIR_GROUP)]
    groups = []
    while any(remaining.values()):
        live = sorted((i for i in remaining if remaining[i]), key=lambda i: -len(remaining[i]))[:PAIR_GROUP]
        group = [(remaining[i].pop(), i) for i in live]
        groups.append(group + padding[len(group):])
    groups.append(padding)
    flat = [pair for group in groups for pair in group]
    return np.asarray([j for j, _ in flat], np.int32), np.asarray([i for _, i in flat], np.int32)


def _attn_kernel(jt_ref, it_ref, q_ref, k_ref, vt_ref, km_ref, o_ref,
                 qa_ref, ka_ref, m_ref, acc_ref, s0_ref, s1_ref, c0_ref, c1_ref):
    n_blocks = q_ref.shape[0] // MOBA_BLOCK
    n_groups = jt_ref.shape[0] // PAIR_GROUP - 1
    heads = range(HEADS_PER_LANE_TILE)
    lane = lax.broadcasted_iota(jnp.int32, (1, LANES), 1)
    head_masks = [(lane // HEAD_DIM) == a for a in heads]
    spare_lane0 = [(HEADS_PER_LANE_TILE - 1 - a) * HEAD_DIM for a in heads]
    km_heads = [jnp.where(hm, km_ref[...], 0.0).astype(BF16) for hm in head_masks]
    blk_row = lax.broadcasted_iota(jnp.int32, (n_blocks, MOBA_BLOCK), 0)
    key_pos = lax.broadcasted_iota(jnp.int32, (MOBA_BLOCK, MOBA_BLOCK), 0)
    qry_pos = lax.broadcasted_iota(jnp.int32, (MOBA_BLOCK, MOBA_BLOCK), 1)
    causal = key_pos <= qry_pos
    def tiles(s):
        return s.reshape(s.shape[0] // SUBLANES, SUBLANES, s.shape[1])

    def over_keys(reduce_pair, s3):
        r = s3[0]
        for t in range(1, s3.shape[0]):
            r = reduce_pair(r, s3[t])
        for shift in (4, 2, 1):
            r = reduce_pair(r, pltpu.roll(r, shift, 0))
        return r

    def init_block(i):
        r0 = pl.multiple_of(i * MOBA_BLOCK, MOBA_BLOCK)
        q_blk = q_ref[pl.ds(r0, MOBA_BLOCK), :]
        k_own = k_ref[pl.ds(r0, MOBA_BLOCK), :]
        for a in heads:
            ka_ref[a, pl.ds(r0, MOBA_BLOCK), :] = jnp.where(
                head_masks[a], k_own, (lane == spare_lane0[a] + i).astype(BF16))
            qa = jnp.where(head_masks[a], q_blk, jnp.zeros_like(q_blk))
            score = lax.dot_general(km_heads[a], qa, _NT, preferred_element_type=F32)
            score = jnp.where(blk_row < i, score, -jnp.inf)
            chosen = jnp.zeros_like(score)
            for _ in range(MOBA_TOP_K):
                best = jnp.max(score, axis=0, keepdims=True)
                first = jnp.min(jnp.where(score == best, blk_row, n_blocks), axis=0, keepdims=True)
                pick = jnp.logical_and(blk_row == first, best > -jnp.inf)
                chosen = jnp.where(pick, 1.0, chosen)
                score = jnp.where(pick, -jnp.inf, score)
            bias = jnp.where(chosen > 0.0, 0.0, NEG_INF)
            bias_rows = [jnp.zeros((spare_lane0[a], MOBA_BLOCK), F32)] if spare_lane0[a] else []
            bias_rows += [bias, jnp.zeros((LANES - spare_lane0[a] - n_blocks, MOBA_BLOCK), F32)]
            bias_lanes = jnp.concatenate(bias_rows, axis=0).T
            qa_ref[a, i] = jnp.where(head_masks[a], q_blk, bias_lanes.astype(BF16))
            s = lax.dot_general(k_own, qa, _NT, preferred_element_type=F32)
            s3 = tiles(jnp.where(causal, s, NEG_INF))
            m = over_keys(jnp.maximum, s3)
            p3 = jnp.exp2(s3 - m[None])
            m_ref[a, i] = m
            acc_ref[a, i] = jnp.dot(vt_ref[a, i], p3.reshape(MOBA_BLOCK, MOBA_BLOCK).astype(BF16),
                                    preferred_element_type=F32)[:STATE_ROWS]

    def init_blocks(h, carry):
        for d in range(INIT_UNROLL):
            init_block(h * INIT_UNROLL + d)
        return carry

    lax.fori_loop(0, n_blocks // INIT_UNROLL, init_blocks, 0)
    for u in range(PAIR_GROUP):
        for a in heads:
            qa_ref[a, n_blocks + u] = jnp.zeros(qa_ref.shape[2:], qa_ref.dtype)
            m_ref[a, n_blocks + u] = jnp.zeros(m_ref.shape[2:], F32)
            acc_ref[a, n_blocks + u] = jnp.zeros(acc_ref.shape[2:], F32)

    def stage_scores(g, u, a, s_ref, c_ref):
        t = g * PAIR_GROUP + u
        c0 = pl.multiple_of(jt_ref[t] * MOBA_BLOCK, MOBA_BLOCK)
        s3 = tiles(lax.dot_general(ka_ref[a, pl.ds(c0, MOBA_BLOCK), :], qa_ref[a, it_ref[t]], _NT,
                                   preferred_element_type=F32))
        s_ref[u, a] = s3
        c_ref[u, a] = over_keys(jnp.maximum, s3)

    def fold_scores(g, u, a, s_ref, c_ref):
        t = g * PAIR_GROUP + u
        j, i = jt_ref[t], it_ref[t]
        m = m_ref[a, i]
        m_new = jnp.maximum(m, c_ref[u, a])
        m_ref[a, i] = m_new
        alpha = jnp.exp2(m - m_new)
        p3 = jnp.exp2(s_ref[u, a] - m_new[None])
        pv = jnp.dot(vt_ref[a, j], p3.reshape(MOBA_BLOCK, MOBA_BLOCK).astype(BF16),
                     preferred_element_type=F32)[:STATE_ROWS]
        acc_ref[a, i] = (alpha[None] * tiles(acc_ref[a, i])).reshape(pv.shape) + pv

    staging = ((s0_ref, c0_ref), (s1_ref, c1_ref))
    for u in range(PAIR_GROUP):
        for a in heads:
            stage_scores(0, u, a, *staging[0])

    def sweep_group(g, carry):
        for parity in range(2):
            @pl.when(g % 2 == parity)
            def _():
                for u in range(PAIR_GROUP):
                    for a in heads:
                        stage_scores(g + 1, u, a, *staging[1 - parity])
                        fold_scores(g, u, a, *staging[parity])
        return carry

    lax.fori_loop(0, n_groups, sweep_group, 0)

    def finish_blocks(h, carry):
        for d in range(INIT_UNROLL):
            i = h * INIT_UNROLL + d
            r0 = pl.multiple_of(i * MOBA_BLOCK, MOBA_BLOCK)
            out_t = jnp.concatenate(
                [(tiles(acc_ref[a, i, :HEAD_DIM]) / acc_ref[a, i, HEAD_DIM:][None]).reshape(HEAD_DIM, MOBA_BLOCK)
                 for a in heads], axis=0)
            o_ref[pl.ds(r0, MOBA_BLOCK), :] = out_t.T.astype(o_ref.dtype)
        return carry

    lax.fori_loop(0, n_blocks // INIT_UNROLL, finish_blocks, 0)


def _mix_kernel(x_ref, attn_ref, u_ref, uh_ref, g1_ref, wg_ref, bg_ref, wp_ref, ps_ref, wba_ref, wbp_ref, wo_ref,
                o_ref, *, tiles_per_seq):
    tm, d = x_ref.shape
    i = pl.program_id(0)
    t_in_seq = (i % tiles_per_seq) * tm
    x = x_ref[...]
    h = (_rms(x) * g1_ref[...]).astype(BF16)
    gates = jax.nn.sigmoid(jnp.dot(h, wg_ref[...], preferred_element_type=F32) + bg_ref[...])

    halo = jnp.where(t_in_seq == 0, 0.0, uh_ref[...])
    ue = jnp.concatenate([halo, u_ref[...]], axis=0)
    pos = (t_in_seq + lax.broadcasted_iota(jnp.int32, (tm, 1), 0)).astype(F32)
    mixed_pool = []
    for g, w in enumerate(POOL_WINDOWS):
        e = ue[:, g * POOL_GROUP_DIM:(g + 1) * POOL_GROUP_DIM]
        win, span = e, 1
        while span < w:
            win = win + pltpu.roll(win, span, 0)
            span *= 2
        count = jnp.minimum(pos + 1.0, float(w))
        pooled = win[POOL_HALO:] / count - e[POOL_HALO:]
        mixed_pool.append(jnp.dot(pooled.astype(BF16), wp_ref[g], preferred_element_type=F32))
    pooled_mix = jnp.concatenate(mixed_pool, axis=1) * ps_ref[...]

    br_attn = jnp.dot(attn_ref[...], wba_ref[...], preferred_element_type=F32)
    br_pool = jnp.dot(pooled_mix.astype(BF16), wbp_ref[...], preferred_element_type=F32)
    mixed = gates[:, :d] * br_attn + gates[:, d:] * br_pool
    o_ref[...] = x + jnp.dot(mixed.astype(BF16), wo_ref[...], preferred_element_type=F32)


def _ffn_kernel(x_ref, xh_ref, g2_ref, wug_ref, wuv_ref, cg_ref, cv_ref, wd_ref, o_ref,
                ug0_ref, uv0_ref, ug1_ref, uv1_ref, act_ref, *, tiles_per_seq):
    tm = x_ref.shape[0]
    n_chunks = wug_ref.shape[0]
    i = pl.program_id(0)
    x = x_ref[...]
    halo = jnp.where(i % tiles_per_seq == 0, 0.0, xh_ref[...])
    h2 = (_rms(jnp.concatenate([halo, x], axis=0)) * g2_ref[...]).astype(BF16)
    staging = [(ug0_ref, uv0_ref), (ug1_ref, uv1_ref)]

    def stage_up(c):
        ug_ref, uv_ref = staging[c % 2]
        ug_ref[...] = jnp.dot(h2, wug_ref[c], preferred_element_type=F32)
        uv_ref[...] = jnp.dot(h2, wuv_ref[c], preferred_element_type=F32)

    def conv(u_ref, cw):
        taps = [u_ref[pl.ds(CONV_HALO - (CONV_WIDTH - 1) + j, tm), :] for j in range(CONV_WIDTH)]
        return taps[0] * cw[0:1] + taps[1] * cw[1:2] + taps[2] * cw[2:3] + cw[3:4]

    def down(c):
        lo, hi = c * FF_CHUNK, (c + 1) * FF_CHUNK
        o_ref[...] += jnp.dot(act_ref[:, lo:hi], wd_ref[lo:hi, :], preferred_element_type=F32)

    o_ref[...] = x
    stage_up(0)
    for c in range(n_chunks):
        if c + 1 < n_chunks:
            stage_up(c + 1)
        if c >= DOWN_LAG:
            down(c - DOWN_LAG)
        ug_ref, uv_ref = staging[c % 2]
        gate = conv(ug_ref, cg_ref[c])
        val = conv(uv_ref, cv_ref[c])
        act_ref[:, c * FF_CHUNK:(c + 1) * FF_CHUNK] = (gate * jax.nn.sigmoid(gate) * val).astype(BF16)
    for c in range(max(n_chunks - DOWN_LAG, 0), n_chunks):
        down(c)


def _const_spec(shape):
    nd = len(shape)
    return pl.BlockSpec(shape, lambda *_: (0,) * nd, pipeline_mode=pl.Buffered(1))


def _rotary_tables(seq):
    inv_freq = ROPE_THETA ** (-jnp.arange(ROPE_HALF, dtype=F32) / ROPE_HALF)
    ang = jnp.arange(seq).astype(F32)[:, None] * inv_freq[None, :]
    cos, sin = jnp.cos(ang), jnp.sin(ang)
    dim = jnp.arange(LANES) % HEAD_DIM
    freq = dim % ROPE_HALF
    cos_t = jnp.where(dim < ROPE_DIM, cos[:, freq], 1.0)
    sin_next = jnp.where(dim < ROPE_HALF, -sin[:, freq], 0.0)
    sin_prev = jnp.where((dim >= ROPE_HALF) & (dim < ROPE_DIM), sin[:, freq], 0.0)
    return cos_t, sin_next, sin_prev


def _layer(x, norm_mix_g, w_in, b_gate, q_norm_g, k_norm_g, w_pool, pool_scale,
           w_branch_attn, w_branch_pool, w_out, norm_ffn_g, w_up, conv_w, conv_b, w_down):
    B, S, D = x.shape
    N = B * S
    d_ff = w_down.shape[0]
    tm = min(TOKEN_TILE, S)
    assert S % tm == 0 and tm % MOBA_BLOCK == 0 and d_ff % FF_CHUNK == 0
    assert S // MOBA_BLOCK <= HEAD_DIM
    assert (S // MOBA_BLOCK) % INIT_UNROLL == 0
    tiles_per_seq = S // tm
    n_blocks = S // MOBA_BLOCK
    blocks_per_tile = tm // MOBA_BLOCK
    cparams = functools.partial(pltpu.CompilerParams, vmem_limit_bytes=VMEM_LIMIT_BYTES)

    w_qkp = jnp.concatenate([w_in[:, :2 * ATTN_WIDTH], w_in[:, 3 * ATTN_WIDTH:3 * ATTN_WIDTH + POOL_WIDTH]],
                            axis=1).astype(BF16)
    w_vt = w_in[:, 2 * ATTN_WIDTH:3 * ATTN_WIDTH].T.astype(BF16)
    w_gate = w_in[:, 3 * ATTN_WIDTH + POOL_WIDTH:].astype(BF16)
    head_of = jnp.arange(MXU_DIM) // HEAD_DIM
    block_diag = jnp.where(head_of[:, None] == head_of[None, :], 1.0 / HEAD_DIM, 0.0).astype(BF16)
    gq = jnp.tile(q_norm_g * (HEAD_DIM ** -0.5 * math.log2(math.e)), ATTN_HEADS)[None, :]
    gk = jnp.tile(k_norm_g, ATTN_HEADS)[None, :]
    cos_t, sin_next, sin_prev = _rotary_tables(S)

    n_tiles = N // tm

    def projected(i):
        j = jnp.minimum(i, n_tiles - 1)
        return j // tiles_per_seq, j % tiles_per_seq

    def finished(i):
        j = jnp.maximum(i - 1, 0)
        return j // tiles_per_seq, j % tiles_per_seq

    table_spec = pl.BlockSpec((tm, LANES), lambda i: (finished(i)[1], 0))
    q, k, vt, u_pool, kmean = pl.pallas_call(
        _proj_kernel,
        grid=(n_tiles + 1,),
        in_specs=[
            pl.BlockSpec((None, tm, D), lambda i: (*projected(i), 0)),
            _const_spec((1, D)),
            _const_spec(w_qkp.shape),
            _const_spec(w_vt.shape),
            _const_spec(block_diag.shape),
            _const_spec(gq.shape),
            _const_spec(gk.shape),
            table_spec, table_spec, table_spec,
        ],
        out_specs=[
            pl.BlockSpec((None, HEAD_PAIRS, tm, LANES), lambda i: (finished(i)[0], 0, finished(i)[1], 0)),
            pl.BlockSpec((None, HEAD_PAIRS, tm, LANES), lambda i: (finished(i)[0], 0, finished(i)[1], 0)),
            pl.BlockSpec((None, ATTN_HEADS, blocks_per_tile, V_ROWS, MOBA_BLOCK),
                         lambda i: (projected(i)[0], 0, projected(i)[1], 0, 0)),
            pl.BlockSpec((None, tm, POOL_WIDTH), lambda i: (*projected(i), 0)),
            pl.BlockSpec((None, None, blocks_per_tile, ATTN_WIDTH), lambda i: (*finished(i), 0, 0)),
        ],
        scratch_shapes=2 * [pltpu.VMEM((tm, 2 * ATTN_WIDTH), F32)],
        out_shape=[
            jax.ShapeDtypeStruct((B, HEAD_PAIRS, S, LANES), BF16),
            jax.ShapeDtypeStruct((B, HEAD_PAIRS, S, LANES), BF16),
            jax.ShapeDtypeStruct((B, ATTN_HEADS, n_blocks, V_ROWS, MOBA_BLOCK), BF16),
            jax.ShapeDtypeStruct((B, S, POOL_WIDTH), F32),
            jax.ShapeDtypeStruct((B, tiles_per_seq, blocks_per_tile, ATTN_WIDTH), F32),
        ],
        compiler_params=cparams(dimension_semantics=("arbitrary",)),
        name="moba_proj",
    )(x, norm_mix_g[None, :], w_qkp, w_vt, block_diag, gq, gk, cos_t, sin_next, sin_prev)

    kmean = kmean.reshape(B, n_blocks, HEAD_PAIRS, LANES).transpose(0, 2, 1, 3)

    key_tab, query_tab = _pair_tables(n_blocks)
    n_slots = n_blocks + PAIR_GROUP
    attn = pl.pallas_call(
        _attn_kernel,
        grid_spec=pltpu.PrefetchScalarGridSpec(
            num_scalar_prefetch=2,
            grid=(B, HEAD_PAIRS),
            in_specs=[
                pl.BlockSpec((None, None, S, LANES), lambda b, p, *_: (b, p, 0, 0)),
                pl.BlockSpec((None, None, S, LANES), lambda b, p, *_: (b, p, 0, 0)),
                pl.BlockSpec((None, HEADS_PER_LANE_TILE, n_blocks, V_ROWS, MOBA_BLOCK),
                             lambda b, p, *_: (b, p, 0, 0, 0)),
                pl.BlockSpec((None, None, n_blocks, LANES), lambda b, p, *_: (b, p, 0, 0)),
            ],
            out_specs=pl.BlockSpec((None, S, LANES), lambda b, p, *_: (b, 0, p)),
            scratch_shapes=[
                pltpu.VMEM((HEADS_PER_LANE_TILE, n_slots, MOBA_BLOCK, LANES), BF16),
                pltpu.VMEM((HEADS_PER_LANE_TILE, S, LANES), BF16),
                pltpu.VMEM((HEADS_PER_LANE_TILE, n_slots, SUBLANES, MOBA_BLOCK), F32),
                pltpu.VMEM((HEADS_PER_LANE_TILE, n_slots, STATE_ROWS, MOBA_BLOCK), F32),
            ] + 2 * [pltpu.VMEM((PAIR_GROUP, HEADS_PER_LANE_TILE, MOBA_BLOCK // SUBLANES, SUBLANES, MOBA_BLOCK),
                                F32)]
              + 2 * [pltpu.VMEM((PAIR_GROUP, HEADS_PER_LANE_TILE, SUBLANES, MOBA_BLOCK), F32)],
        ),
        out_shape=jax.ShapeDtypeStruct((B, S, ATTN_WIDTH), BF16),
        compiler_params=cparams(dimension_semantics=("parallel", "parallel")),
        name="moba_attn",
    )(key_tab, query_tab, q, k, vt, kmean)

    x2 = x.reshape(N, D)
    x1 = pl.pallas_call(
        functools.partial(_mix_kernel, tiles_per_seq=tiles_per_seq),
        grid=(N // tm,),
        in_specs=[
            pl.BlockSpec((tm, D), lambda i: (i, 0)),
            pl.BlockSpec((tm, ATTN_WIDTH), lambda i: (i, 0)),
            pl.BlockSpec((tm, POOL_WIDTH), lambda i: (i, 0)),
            pl.BlockSpec((POOL_HALO, POOL_WIDTH), lambda i: (jnp.maximum(i * (tm // POOL_HALO) - 1, 0), 0)),
            _const_spec((1, D)),
            _const_spec(w_gate.shape),
            _const_spec((1, 2 * D)),
            _const_spec(w_pool.shape),
            _const_spec((1, POOL_WIDTH)),
            _const_spec(w_branch_attn.shape),
            _const_spec(w_branch_pool.shape),
            _const_spec(w_out.shape),
        ],
        out_specs=pl.BlockSpec((tm, D), lambda i: (i, 0)),
        out_shape=jax.ShapeDtypeStruct((N, D), F32),
        compiler_params=cparams(dimension_semantics=("parallel",)),
        name="moba_mix",
    )(x2, attn.reshape(N, ATTN_WIDTH), u_pool.reshape(N, POOL_WIDTH), u_pool.reshape(N, POOL_WIDTH),
      norm_mix_g[None, :], w_gate, b_gate[None, :], w_pool.astype(BF16), pool_scale[None, :],
      w_branch_attn.astype(BF16), w_branch_pool.astype(BF16), w_out.astype(BF16))

    n_chunks = d_ff // FF_CHUNK
    tf = min(FFN_TILE, S)
    assert S % tf == 0

    def chunked_cols(w):
        return w.reshape(w.shape[0], n_chunks, FF_CHUNK).transpose(1, 0, 2)

    conv_all = jnp.concatenate([conv_w, conv_b[None, :]], axis=0)
    out = pl.pallas_call(
        functools.partial(_ffn_kernel, tiles_per_seq=S // tf),
        grid=(N // tf,),
        in_specs=[
            pl.BlockSpec((tf, D), lambda i: (i, 0)),
            pl.BlockSpec((CONV_HALO, D), lambda i: (jnp.maximum(i * (tf // CONV_HALO) - 1, 0), 0)),
            _const_spec((1, D)),
            _const_spec((n_chunks, D, FF_CHUNK)),
            _const_spec((n_chunks, D, FF_CHUNK)),
            _const_spec((n_chunks, CONV_WIDTH + 1, FF_CHUNK)),
            _const_spec((n_chunks, CONV_WIDTH + 1, FF_CHUNK)),
            _const_spec((d_ff, D)),
        ],
        out_specs=pl.BlockSpec((tf, D), lambda i: (i, 0)),
        out_shape=jax.ShapeDtypeStruct((N, D), F32),
        scratch_shapes=4 * [pltpu.VMEM((CONV_HALO + tf, FF_CHUNK), F32)]
                       + [pltpu.VMEM((tf, d_ff), BF16)],
        compiler_params=cparams(dimension_semantics=("parallel",)),
        name="moba_ffn",
    )(x1, x1, norm_ffn_g[None, :],
      chunked_cols(w_up[:, :d_ff]).astype(BF16), chunked_cols(w_up[:, d_ff:]).astype(BF16),
      chunked_cols(conv_all[:, :d_ff]), chunked_cols(conv_all[:, d_ff:]),
      w_down.astype(BF16))
    return out.reshape(B, S, D)


def kernel(x, norm_mix_g, w_in, b_gate, q_norm_g, k_norm_g, w_pool, pool_scale, w_branch_attn, w_branch_pool,
           w_out, norm_ffn_g, w_up, conv_w, conv_b, w_down):
    for l in range(w_in.shape[0]):
        x = _layer(x, norm_mix_g[l], w_in[l], b_gate[l], q_norm_g[l], k_norm_g[l], w_pool[l], pool_scale[l],
                   w_branch_attn[l], w_branch_pool[l], w_out[l], norm_ffn_g[l], w_up[l], conv_w[l], conv_b[l],
                   w_down[l])
    return x
```
